```python
import math
import jax, jax.numpy as jnp
from jax import lax
import numpy as np

D_MODEL = 1024
BATCH = 2
SEQ = 8192
DEPTH = 2

GRID_W = 64
CTX_LEN = 256
HEAD_DIM = 64
A_HEADS = 6
A_KV_HEADS = 2
A_GROUP = A_HEADS // A_KV_HEADS
WINDOW = 128
B_HEADS = 4
B_QK_DIM = 32
B_V_DIM = 2 * B_QK_DIM
C_HEADS = 6
C_Q_RANK = 256
C_KV_RANK = 128
C_NOPE = 64
C_ROPE = 32
C_V = 64
MIX_WIDTH = A_HEADS * HEAD_DIM + B_HEADS * B_V_DIM + C_HEADS * C_V
D_FF = 2816
Q_BLOCK = 128
ROPE_BASE = 10000.0
NORM_EPS = 1e-6
NEG_INF = -1e30
A_SCALE = HEAD_DIM ** -0.5
B_SCALE = B_QK_DIM ** -0.5
C_SCALE = (C_NOPE + C_ROPE) ** -0.5
IN_SPLITS = (A_HEADS * HEAD_DIM, A_KV_HEADS * HEAD_DIM, A_KV_HEADS * HEAD_DIM,
             B_HEADS * 2 * B_QK_DIM, B_HEADS * 2 * B_QK_DIM, B_HEADS * B_V_DIM,
             C_Q_RANK, C_KV_RANK, C_ROPE)
D_IN = (A_HEADS * HEAD_DIM + 2 * A_KV_HEADS * HEAD_DIM + 2 * B_HEADS * 2 * B_QK_DIM
        + B_HEADS * B_V_DIM + C_Q_RANK + C_KV_RANK + C_ROPE)

kernel_name = "hybrid_dit_parallel_head_groups"


def rms_norm(x, g):
    xf = x.astype(jnp.float32)
    y = xf * lax.rsqrt(jnp.mean(xf * xf, axis=-1, keepdims=True) + NORM_EPS)
    return (y * g.astype(jnp.float32)).astype(x.dtype)


def modulate(h, shift, scale):
    return h * (1 + scale) + shift


def _rope_1d(x, pos):
    r = x.shape[-1]
    freqs = ROPE_BASE ** (-jnp.arange(0, r, 2, dtype=jnp.float32) / r)
    ang = pos[:, None] * freqs[None, :]
    shape = (1, x.shape[1]) + (1,) * (x.ndim - 3) + (r // 2,)
    cos = jnp.cos(ang).reshape(shape)
    sin = jnp.sin(ang).reshape(shape)
    xf = x.astype(jnp.float32)
    x1, x2 = jnp.split(xf, 2, axis=-1)
    return jnp.concatenate([x1 * cos - x2 * sin, x2 * cos + x1 * sin], axis=-1).astype(x.dtype)


def axial_rope(x, row, col):
    half = x.shape[-1] // 2
    return jnp.concatenate([_rope_1d(x[..., :half], row), _rope_1d(x[..., half:], col)], axis=-1)


def split_in(p):
    outs = []
    start = 0
    for size in IN_SPLITS:
        outs.append(p[..., start:start + size])
        start += size
    return outs


def sweep_query_blocks(fn, q):
    B, n = q.shape[0], q.shape[1]
    nb = n // Q_BLOCK
    qb = jnp.moveaxis(q.reshape((B, nb, Q_BLOCK) + q.shape[2:]), 1, 0)
    out = jnp.moveaxis(lax.map(fn, qb), 0, 1)
    return out.reshape((B, n) + out.shape[3:])


def dense_attend(q, k, v, scale):
    s = jnp.einsum("bqhd,bkhd->bhqk", q, k).astype(jnp.float32) * scale
    p = jax.nn.softmax(s, axis=-1)
    return jnp.einsum("bhqk,bkhd->bqhd", p.astype(v.dtype), v)


def diff_attend(q, k, v, lam):
    s = jnp.einsum("bqhmd,bkhmd->bhmqk", q, k).astype(jnp.float32) * B_SCALE
    p = jax.nn.softmax(s, axis=-1)
    a = p[:, :, 0] - lam * p[:, :, 1]
    return jnp.einsum("bhqk,bkhd->bqhd", a.astype(v.dtype), v)


def windowed_gqa_latent(q, k, v, k_ctx, v_ctx, sink):
    B, S = q.shape[0], q.shape[1]
    nb = S // WINDOW
    qb = q.reshape(B, nb, WINDOW, A_KV_HEADS, A_GROUP, HEAD_DIM)
    pad = ((0, 0), (WINDOW, WINDOW), (0, 0), (0, 0))
    kp = jnp.pad(k, pad).reshape(B, nb + 2, WINDOW, A_KV_HEADS, HEAD_DIM)
    vp = jnp.pad(v, pad).reshape(B, nb + 2, WINDOW, A_KV_HEADS, HEAD_DIM)
    k_band = jnp.concatenate([kp[:, :-2], kp[:, 1:-1], kp[:, 2:]], axis=2)
    v_band = jnp.concatenate([vp[:, :-2], vp[:, 1:-1], vp[:, 2:]], axis=2)
    s_loc = jnp.einsum("bnqkgd,bnjkd->bnkgqj", qb, k_band).astype(jnp.float32) * A_SCALE
    qi = jnp.arange(WINDOW)
    kj = jnp.arange(3 * WINDOW)
    blk = jnp.arange(nb)
    offset = kj[None, :] - WINDOW - qi[:, None]
    key_pos = blk[:, None] * WINDOW - WINDOW + kj[None, :]
    mask = (jnp.abs(offset) <= WINDOW)[None] & ((key_pos >= 0) & (key_pos < S))[:, None, :]
    s_loc = jnp.where(mask[None, :, None, None], s_loc, NEG_INF)
    s_ctx = jnp.einsum("bnqkgd,bckd->bnkgqc", qb, k_ctx).astype(jnp.float32) * A_SCALE
    sink_col = jnp.broadcast_to(sink.astype(jnp.float32).reshape(1, 1, A_KV_HEADS, A_GROUP, 1, 1),
                                s_loc.shape[:-1] + (1,))
    p = jax.nn.softmax(jnp.concatenate([s_loc, s_ctx, sink_col], axis=-1), axis=-1).astype(v.dtype)
    n_loc = 3 * WINDOW
    n_ctx = k_ctx.shape[1]
    o = (jnp.einsum("bnkgqj,bnjkd->bnqkgd", p[..., :n_loc], v_band)
         + jnp.einsum("bnkgqc,bckd->bnqkgd", p[..., n_loc:n_loc + n_ctx], v_ctx))
    return o.reshape(B, S, A_HEADS * HEAD_DIM)


def gqa_context(q, k, v, sink):
    B, C = q.shape[0], q.shape[1]
    qg = q.reshape(B, C, A_KV_HEADS, A_GROUP, HEAD_DIM)
    s = jnp.einsum("bqkgd,bckd->bkgqc", qg, k).astype(jnp.float32) * A_SCALE
    sink_col = jnp.broadcast_to(sink.astype(jnp.float32).reshape(1, A_KV_HEADS, A_GROUP, 1, 1),
                                s.shape[:-1] + (1,))
    p = jax.nn.softmax(jnp.concatenate([s, sink_col], axis=-1), axis=-1).astype(v.dtype)
    o = jnp.einsum("bkgqc,bckd->bqkgd", p[..., :C], v)
    return o.reshape(B, C, A_HEADS * HEAD_DIM)


def mla_qkv(c_q, c_kv, k_r, q_norm, w_q_up, kv_norm, w_kv_up, row, col, use_rope):
    B, n = c_q.shape[0], c_q.shape[1]
    q = (rms_norm(c_q, q_norm) @ w_q_up).reshape(B, n, C_HEADS, C_NOPE + C_ROPE)
    q_nope, q_rope = q[..., :C_NOPE], q[..., C_NOPE:]
    kv = (rms_norm(c_kv, kv_norm) @ w_kv_up).reshape(B, n, C_HEADS, C_NOPE + C_V)
    k_nope, v = kv[..., :C_NOPE], kv[..., C_NOPE:]
    if use_rope:
        q_rope = axial_rope(q_rope, row, col)
        k_r = axial_rope(k_r, row, col)
    k = jnp.concatenate([k_nope, jnp.broadcast_to(k_r[:, :, None, :], (B, n, C_HEADS, C_ROPE))], axis=-1)
    q = jnp.concatenate([q_nope, q_rope], axis=-1)
    return q, k, v


def swiglu(h, w_gate, w_up, w_down):
    return (jax.nn.silu(h @ w_gate) * (h @ w_up)) @ w_down


def hybrid_layer(l, update_ctx, x, ctx, mod_x, mod_c, row, col,
                 g_pre_mix, g_post_mix, w_in, win_sink,
                 diff_lambda_q1, diff_lambda_k1, diff_lambda_q2, diff_lambda_k2, diff_sub_norm,
                 mla_q_norm, mla_w_q_up, mla_kv_norm, mla_w_kv_up,
                 w_out, g_pre_ffn, g_post_ffn, w_gate, w_up, w_down):
    B, S = x.shape[0], x.shape[1]
    C = ctx.shape[1]
    sh_m, sc_m, gt_m, sh_f, sc_f, gt_f = jnp.split(mod_x, 6, axis=-1)
    csh_m, csc_m, cgt_m, csh_f, csc_f, cgt_f = jnp.split(mod_c, 6, axis=-1)

    h_x = modulate(rms_norm(x, g_pre_mix), sh_m, sc_m)
    h_c = modulate(rms_norm(ctx, g_pre_mix), csh_m, csc_m)
    ax_q, ax_k, ax_v, dx_q, dx_k, dx_v, mx_q, mx_kv, mx_kr = split_in(h_x @ w_in)
    ac_q, ac_k, ac_v, dc_q, dc_k, dc_v, mc_q, mc_kv, mc_kr = split_in(h_c @ w_in)

    qa_x = axial_rope(ax_q.reshape(B, S, A_HEADS, HEAD_DIM), row, col)
    ka_x = axial_rope(ax_k.reshape(B, S, A_KV_HEADS, HEAD_DIM), row, col)
    va_x = ax_v.reshape(B, S, A_KV_HEADS, HEAD_DIM)
    ka_c = ac_k.reshape(B, C, A_KV_HEADS, HEAD_DIM)
    va_c = ac_v.reshape(B, C, A_KV_HEADS, HEAD_DIM)
    o_a = windowed_gqa_latent(qa_x, ka_x, va_x, ka_c, va_c, win_sink)

    lam_init = 0.8 - 0.6 * math.exp(-0.3 * l)
    lam = (jnp.exp(jnp.sum(diff_lambda_q1.astype(jnp.float32) * diff_lambda_k1.astype(jnp.float32)))
           - jnp.exp(jnp.sum(diff_lambda_q2.astype(jnp.float32) * diff_lambda_k2.astype(jnp.float32)))
           + lam_init)
    qd_x = axial_rope(dx_q.reshape(B, S, B_HEADS, 2, B_QK_DIM), row, col)
    kd_x = axial_rope(dx_k.reshape(B, S, B_HEADS, 2, B_QK_DIM), row, col)
    vd_x = dx_v.reshape(B, S, B_HEADS, B_V_DIM)
    kd_c = dc_k.reshape(B, C, B_HEADS, 2, B_QK_DIM)
    vd_c = dc_v.reshape(B, C, B_HEADS, B_V_DIM)
    kd_all = jnp.concatenate([kd_x, kd_c], axis=1)
    vd_all = jnp.concatenate([vd_x, vd_c], axis=1)
    o_b = sweep_query_blocks(lambda qb: diff_attend(qb, kd_all, vd_all, lam), qd_x)
    o_b = (rms_norm(o_b, diff_sub_norm) * (1.0 - lam_init)).reshape(B, S, B_HEADS * B_V_DIM)

    qm_x, km_x, vm_x = mla_qkv(mx_q, mx_kv, mx_kr, mla_q_norm, mla_w_q_up, mla_kv_norm, mla_w_kv_up,
                               row, col, True)
    qm_c, km_c, vm_c = mla_qkv(mc_q, mc_kv, mc_kr, mla_q_norm, mla_w_q_up, mla_kv_norm, mla_w_kv_up,
                               None, None, False)
    km_all = jnp.concatenate([km_x, km_c], axis=1)
    vm_all = jnp.concatenate([vm_x, vm_c], axis=1)
    o_c = sweep_query_blocks(lambda qb: dense_attend(qb, km_all, vm_all, C_SCALE), qm_x)
    o_c = o_c.reshape(B, S, C_HEADS * C_V)

    mix_x = jnp.concatenate([o_a, o_b, o_c], axis=-1) @ w_out
    x = x + gt_m * rms_norm(mix_x, g_post_mix)
    h = modulate(rms_norm(x, g_pre_ffn), sh_f, sc_f)
    x = x + gt_f * rms_norm(swiglu(h, w_gate, w_up, w_down), g_post_ffn)

    if update_ctx:
        o_a_c = gqa_context(ac_q.reshape(B, C, A_HEADS, HEAD_DIM), ka_c, va_c, win_sink)
        o_b_c = diff_attend(dc_q.reshape(B, C, B_HEADS, 2, B_QK_DIM), kd_c, vd_c, lam)
        o_b_c = (rms_norm(o_b_c, diff_sub_norm) * (1.0 - lam_init)).reshape(B, C, B_HEADS * B_V_DIM)
        o_c_c = dense_attend(qm_c, km_c, vm_c, C_SCALE).reshape(B, C, C_HEADS * C_V)
        mix_c = jnp.concatenate([o_a_c, o_b_c, o_c_c], axis=-1) @ w_out
        ctx = ctx + cgt_m * rms_norm(mix_c, g_post_mix)
        hc = modulate(rms_norm(ctx, g_pre_ffn), csh_f, csc_f)
        ctx = ctx + cgt_f * rms_norm(swiglu(hc, w_gate, w_up, w_down), g_post_ffn)
    return x, ctx


def setup_inputs(seed: int = 0) -> dict:
    key = jax.random.key(seed)
    ks = jax.random.split(key, 26)

    def normal(k, shape, scale):
        return scale * jax.random.normal(k, shape, jnp.float32)

    def gain(k, shape):
        return 1.0 + 0.05 * jax.random.normal(k, shape, jnp.float32)

    return {
        "x": normal(ks[0], (BATCH, SEQ, D_MODEL), 1.0),
        "c": normal(ks[1], (BATCH, D_MODEL), 1.0),
        "ctx": normal(ks[2], (BATCH, CTX_LEN, D_MODEL), 1.0),
        "c_ctx": normal(ks[3], (D_MODEL,), 1.0),
        "w_ada": normal(ks[4], (DEPTH, D_MODEL, 6 * D_MODEL), 0.5 * D_MODEL ** -0.5),
        "b_ada": normal(ks[5], (DEPTH, 6 * D_MODEL), 0.02),
        "g_pre_mix": gain(ks[6], (DEPTH, D_MODEL)),
        "g_post_mix": gain(ks[7], (DEPTH, D_MODEL)),
        "w_in": normal(ks[8], (DEPTH, D_MODEL, D_IN), D_MODEL ** -0.5),
        "win_sink": normal(ks[9], (DEPTH, A_HEADS), 0.5),
        "diff_lambda_q1": normal(ks[10], (DEPTH, B_QK_DIM), 0.1),
        "diff_lambda_k1": normal(ks[11], (DEPTH, B_QK_DIM), 0.1),
        "diff_lambda_q2": normal(ks[12], (DEPTH, B_QK_DIM), 0.1),
        "diff_lambda_k2": normal(ks[13], (DEPTH, B_QK_DIM), 0.1),
        "diff_sub_norm": gain(ks[14], (DEPTH, B_V_DIM)),
        "mla_q_norm": gain(ks[15], (DEPTH, C_Q_RANK)),
        "mla_w_q_up": normal(ks[16], (DEPTH, C_Q_RANK, C_HEADS * (C_NOPE + C_ROPE)), C_Q_RANK ** -0.5),
        "mla_kv_norm": gain(ks[17], (DEPTH, C_KV_RANK)),
        "mla_w_kv_up": normal(ks[18], (DEPTH, C_KV_RANK, C_HEADS * (C_NOPE + C_V)), C_KV_RANK ** -0.5),
        "w_out": normal(ks[19], (DEPTH, MIX_WIDTH, D_MODEL), MIX_WIDTH ** -0.5),
        "g_pre_ffn": gain(ks[20], (DEPTH, D_MODEL)),
        "g_post_ffn": gain(ks[21], (DEPTH, D_MODEL)),
        "w_gate": normal(ks[22], (DEPTH, D_MODEL, D_FF), D_MODEL ** -0.5),
        "w_up": normal(ks[23], (DEPTH, D_MODEL, D_FF), D_MODEL ** -0.5),
        "w_down": normal(ks[24], (DEPTH, D_FF, D_MODEL), D_FF ** -0.5),
    }


def reference(x, c, ctx, c_ctx, w_ada, b_ada, g_pre_mix, g_post_mix, w_in, win_sink,
              diff_lambda_q1, diff_lambda_k1, diff_lambda_q2, diff_lambda_k2, diff_sub_norm,
              mla_q_norm, mla_w_q_up, mla_kv_norm, mla_w_kv_up, w_out,
              g_pre_ffn, g_post_ffn, w_gate, w_up, w_down):
    n = x.shape[1]
    rows = n // GRID_W
    row = jnp.repeat(jnp.arange(rows, dtype=jnp.float32), GRID_W)
    col = jnp.tile(jnp.arange(GRID_W, dtype=jnp.float32), rows)
    for l in range(DEPTH):
        mod_x = (jax.nn.silu(c) @ w_ada[l] + b_ada[l])[:, None, :]
        mod_c = (jax.nn.silu(c_ctx) @ w_ada[l] + b_ada[l])[None, None, :]
        x, ctx = hybrid_layer(l, l < DEPTH - 1, x, ctx, mod_x, mod_c, row, col,
                              g_pre_mix[l], g_post_mix[l], w_in[l], win_sink[l],
                              diff_lambda_q1[l], diff_lambda_k1[l], diff_lambda_q2[l], diff_lambda_k2[l],
                              diff_sub_norm[l], mla_q_norm[l], mla_w_q_up[l], mla_kv_norm[l],
                              mla_w_kv_up[l], w_out[l], g_pre_ffn[l], g_post_ffn[l],
                              w_gate[l], w_up[l], w_down[l])
    return x
```

```python
import functools
import math

import numpy as np
import jax
import jax.numpy as jnp
from jax import lax
from jax.experimental import pallas as pl
from jax.experimental.pallas import tpu as pltpu

D_MODEL = 1024
GRID_W = 64
HEAD_DIM = 64
A_HEADS = 6
A_KV_HEADS = 2
A_GROUP = A_HEADS // A_KV_HEADS
WINDOW = 128
B_HEADS = 4
B_QK_DIM = 32
B_V_DIM = 64
C_HEADS = 6
C_Q_RANK = 256
C_KV_RANK = 128
C_NOPE = 64
C_ROPE = 32
C_V = 64
D_FF = 2816
ROPE_BASE = 10000.0
NORM_EPS = 1e-6
NEG_INF = -1e30
A_SCALE = HEAD_DIM ** -0.5
B_SCALE = B_QK_DIM ** -0.5
C_SCALE = (C_NOPE + C_ROPE) ** -0.5

LANES = 128
TQ = 256
TK = 256
BAND = TQ + 2 * WINDOW
VMEM_LIMIT = 56 * 1024 * 1024

_AQ, _AK, _AV = 0, 384, 512
_DQ, _DK, _DV = 640, 896, 1152
_MQ, _MKV, _MKR = 1408, 1664, 1792
_ZERO_COL = 1824

N_ROPED = 384 + 256 + 256 + 256 + 128
N_PLAIN = 256 + 256 + 256 + 128
N_EXT = 2 * N_ROPED + N_PLAIN


def _rope_partner(unit):
    half = unit // 2
    quarter = half // 2
    d = np.arange(unit)
    r = d % half
    partner = np.where(r < quarter, d + quarter, d - quarter)
    sign = np.where(r < quarter, -1.0, 1.0).astype(np.float32)
    return partner, sign


def _ext_columns():
    pa, _ = _rope_partner(HEAD_DIM)
    pb, _ = _rope_partner(B_QK_DIM)

    def rot(src, unit, partner):
        return (src // unit) * unit + partner[src % unit]

    qa = np.arange(A_HEADS * HEAD_DIM)
    dup = np.concatenate([np.arange(64), np.arange(64), 64 + np.arange(64), 64 + np.arange(64)])
    qb = np.arange(256)
    kr = np.concatenate([np.arange(32), np.arange(32)])
    zeros64 = np.full((64,), _ZERO_COL)
    main = [_AQ + qa, _AK + dup, _DQ + qb, _DK + qb, np.concatenate([_MKR + kr, zeros64])]
    rots = [_AQ + rot(qa, 64, pa), _AK + rot(dup, 64, pa), _DQ + rot(qb, 32, pb), _DK + rot(qb, 32, pb),
            np.concatenate([_MKR + rot(kr, 32, pb), zeros64])]
    plain = [_AV + dup, _DV + qb, _MQ + np.arange(256), _MKV + np.arange(128)]
    cols = np.concatenate(main + rots + plain)
    assert cols.shape[0] == N_EXT
    return cols


def _qup_columns():
    pb, _ = _rope_partner(C_ROPE)
    width = C_NOPE + C_ROPE
    zero = C_HEADS * width
    zeros64 = np.full((64,), zero)
    main, rots = [], []
    for p in range(C_HEADS // 2):
        h0, h1 = 2 * p, 2 * p + 1
        main += [h0 * width + np.arange(64), h1 * width + np.arange(64),
                 h0 * width + 64 + np.arange(32), h1 * width + 64 + np.arange(32), zeros64]
        rots += [h0 * width + 64 + pb, h1 * width + 64 + pb, zeros64]
    return np.concatenate(main + rots)


def _kvup_columns():
    width = C_NOPE + C_V
    k = np.concatenate([h * width + np.arange(64) for h in range(C_HEADS)])
    v = np.concatenate([h * width + 64 + np.arange(64) for h in range(C_HEADS)])
    return np.concatenate([k, v])


def _rope_tables(n_lat, n_ctx):
    t = jnp.arange(n_lat)
    row = (t // GRID_W).astype(jnp.float32)
    col = (t % GRID_W).astype(jnp.float32)

    def table(unit):
        half = unit // 2
        quarter = half // 2
        _, sign = _rope_partner(unit)
        lane = np.arange(LANES)
        d = lane % unit
        axis = d // half
        j = d % quarter
        freqs = ROPE_BASE ** (-jnp.arange(0, half, 2, dtype=jnp.float32) / half)
        f = freqs[j]
        pos = jnp.where(jnp.asarray(axis)[None, :] == 0, row[:, None], col[:, None])
        ang = pos * f[None, :]
        cos = jnp.cos(ang)
        sin = jnp.sin(ang) * jnp.asarray(sign[d])[None, :]
        cos = jnp.concatenate([cos, jnp.ones((n_ctx, LANES), jnp.float32)], axis=0)
        sin = jnp.concatenate([sin, jnp.zeros((n_ctx, LANES), jnp.float32)], axis=0)
        return cos, sin

    cos_a, sin_a = table(HEAD_DIM)
    cos_b, sin_b = table(B_QK_DIM)
    return cos_a, sin_a, cos_b, sin_b


def _rms(x, g):
    return x * lax.rsqrt(jnp.mean(x * x, axis=-1, keepdims=True) + NORM_EPS) * g


def _dot(a, b):
    return jnp.dot(a, b, preferred_element_type=jnp.float32)


def _dot_nt(a, b):
    return lax.dot_general(a, b, (((1,), (1,)), ((), ())), preferred_element_type=jnp.float32)


def _ada_kernel(c_ref, w_ref, b_ref, o_ref):
    c = c_ref[...]
    h = c * jax.nn.sigmoid(c)
    o_ref[0] = _dot(h, w_ref[0]) + b_ref[0]


def _ada_call(cvec, w_ada, b_ada):
    depth, d, n = w_ada.shape
    tn = 1536
    return pl.pallas_call(
        _ada_kernel,
        grid=(depth, n // tn),
        in_specs=[pl.BlockSpec((8, d), lambda l, j: (0, 0)),
                  pl.BlockSpec((1, d, tn), lambda l, j: (l, 0, j)),
                  pl.BlockSpec((1, 1, tn), lambda l, j: (l, 0, j))],
        out_specs=pl.BlockSpec((1, 8, tn), lambda l, j: (l, 0, j)),
        out_shape=jax.ShapeDtypeStruct((depth, 8, n), jnp.float32),
        compiler_params=pltpu.CompilerParams(dimension_semantics=("arbitrary", "arbitrary"),
                                             vmem_limit_bytes=VMEM_LIMIT),
        name="adaln",
    )(cvec, w_ada, b_ada.reshape(depth, 1, n))


def _proj_kernel(x_ref, mod_ref, gpre_ref, w_ref, qn_ref, wq_ref, kvn_ref, wkv_ref,
                 ca_ref, sa_ref, cb_ref, sb_ref,
                 qa_ref, ka_ref, va_ref, qb_ref, kbt_ref, vb_ref, qc_ref, kct_ref, vc_ref):
    mod = mod_ref[0]
    x = x_ref[0]
    h = _rms(x, gpre_ref[...]) * (1.0 + mod[1:2]) + mod[0:1]
    p = _dot(h.astype(jnp.bfloat16), w_ref[...])

    ca, sa, cb, sb = ca_ref[...], sa_ref[...], cb_ref[...], sb_ref[...]

    def roped(off, width, cos, sin):
        n = width // LANES
        main = p[:, off:off + width]
        rot = p[:, N_ROPED + off:N_ROPED + off + width]
        return main * jnp.concatenate([cos] * n, axis=1) + rot * jnp.concatenate([sin] * n, axis=1)

    qa_ref[0] = (roped(0, 384, ca, sa) * A_SCALE).astype(jnp.bfloat16)
    ka_ref[0] = roped(384, 256, ca, sa).astype(jnp.bfloat16)
    qb_ref[0] = (roped(640, 256, cb, sb) * B_SCALE).astype(jnp.bfloat16)
    kbt_ref[0, 0] = roped(896, 256, cb, sb).T.astype(jnp.bfloat16)
    kr = roped(1152, 128, cb, sb)

    o = 2 * N_ROPED
    va_ref[0] = p[:, o:o + 256].astype(jnp.bfloat16)
    vb_ref[0] = p[:, o + 256:o + 512].astype(jnp.bfloat16)
    cq = p[:, o + 512:o + 768]
    ckv = p[:, o + 768:o + 896]

    qc = _dot(_rms(cq, qn_ref[...]).astype(jnp.bfloat16), wq_ref[...])
    kv = _dot(_rms(ckv, kvn_ref[...]).astype(jnp.bfloat16), wkv_ref[...])
    vc_ref[0] = kv[:, 384:].astype(jnp.bfloat16)
    kr_t = kr.T.astype(jnp.bfloat16)
    for pr in range(C_HEADS // 2):
        nope = qc[:, 256 * pr:256 * pr + 128]
        rope = (qc[:, 256 * pr + 128:256 * pr + 256] * cb
                + qc[:, 768 + 128 * pr:768 + 128 * pr + 128] * sb)
        qc_ref[0, :, 256 * pr:256 * pr + 128] = (nope * C_SCALE).astype(jnp.bfloat16)
        qc_ref[0, :, 256 * pr + 128:256 * pr + 256] = (rope * C_SCALE).astype(jnp.bfloat16)
        kct_ref[0, 0, 256 * pr:256 * pr + 128, :] = kv[:, 128 * pr:128 * pr + 128].T.astype(jnp.bfloat16)
        kct_ref[0, 0, 256 * pr + 128:256 * pr + 256, :] = kr_t


def _const_spec(shape):
    nd = len(shape)
    return pl.BlockSpec(shape, lambda *_: (0,) * nd, pipeline_mode=pl.Buffered(1))


def _mod_spec(n_lat_blocks):
    return pl.BlockSpec((1, 6, D_MODEL), lambda i, b: (2 * b + (i >= n_lat_blocks).astype(jnp.int32), 0, 0))


def _proj_call(xs, mods, gpre, w_ext, qn, wq, kvn, wkv, tables, n_lat_blocks):
    bsz, t, d = xs.shape
    nb = t // TQ
    tok = lambda w: pl.BlockSpec((1, TQ, w), lambda i, b: (b, i, 0))
    tbl = pl.BlockSpec((TQ, LANES), lambda i, b: (i, 0))
    kt = lambda r: pl.BlockSpec((1, 1, r, TQ), lambda i, b: (b, i, 0, 0))
    bf = jnp.bfloat16
    out_shape = [jax.ShapeDtypeStruct((bsz, t, 384), bf), jax.ShapeDtypeStruct((bsz, t, 256), bf),
                 jax.ShapeDtypeStruct((bsz, t, 256), bf), jax.ShapeDtypeStruct((bsz, t, 256), bf),
                 jax.ShapeDtypeStruct((bsz, nb, 256, TQ), bf), jax.ShapeDtypeStruct((bsz, t, 256), bf),
                 jax.ShapeDtypeStruct((bsz, t, 768), bf), jax.ShapeDtypeStruct((bsz, nb, 768, TQ), bf),
                 jax.ShapeDtypeStruct((bsz, t, 384), bf)]
    out_specs = [tok(384), tok(256), tok(256), tok(256), kt(256), tok(256), tok(768), kt(768), tok(384)]
    return pl.pallas_call(
        _proj_kernel,
        grid=(nb, bsz),
        in_specs=[tok(d), _mod_spec(n_lat_blocks), _const_spec((1, d)), _const_spec(w_ext.shape),
                  _const_spec(qn.shape), _const_spec(wq.shape), _const_spec(kvn.shape), _const_spec(wkv.shape),
                  tbl, tbl, tbl, tbl],
        out_specs=out_specs,
        out_shape=out_shape,
        compiler_params=pltpu.CompilerParams(dimension_semantics=("arbitrary", "arbitrary"),
                                             vmem_limit_bytes=VMEM_LIMIT),
        name="proj",
    )(xs, mods, gpre, w_ext, qn, wq, kvn, wkv, *tables)


def _attn_a_kernel(sink_ref, q_ref, k_ref, v_ref, o_ref, *, n_lat):
    i = pl.program_id(1)
    is_ctx = i == n_lat // TQ
    t0 = i * TQ
    start = pl.multiple_of(jnp.clip(t0 - WINDOW, 0, n_lat - BAND), WINDOW)
    kband = k_ref[0, pl.ds(start, BAND), :]
    vband = v_ref[0, pl.ds(start, BAND), :]
    kctx = k_ref[0, n_lat:, :]
    vctx = v_ref[0, n_lat:, :]
    qpos = t0 + jnp.where(is_ctx, 4 * n_lat, 0) + lax.broadcasted_iota(jnp.int32, (TQ, BAND), 0)
    kpos = start + lax.broadcasted_iota(jnp.int32, (TQ, BAND), 1)
    mask = jnp.abs(qpos - kpos) <= WINDOW
    low_half = lax.broadcasted_iota(jnp.int32, (TQ, LANES), 1) < HEAD_DIM

    for pr in range(A_HEADS // 2):
        qg = q_ref[0, :, LANES * pr:LANES * (pr + 1)]
        outs = []
        for e in range(2):
            head = 2 * pr + e
            kv = head // A_GROUP
            sel = slice(LANES * kv, LANES * (kv + 1))
            qm = jnp.where(low_half if e == 0 else jnp.logical_not(low_half), qg, jnp.zeros_like(qg))
            s_loc = jnp.where(mask, _dot_nt(qm, kband[:, sel]), NEG_INF)
            s_ctx = _dot_nt(qm, kctx[:, sel])
            sink = sink_ref[head]
            m = jnp.maximum(jnp.maximum(jnp.max(s_loc, axis=-1, keepdims=True),
                                        jnp.max(s_ctx, axis=-1, keepdims=True)), sink)
            p_loc = jnp.exp(s_loc - m)
            p_ctx = jnp.exp(s_ctx - m)
            l = (jnp.sum(p_loc, axis=-1, keepdims=True) + jnp.sum(p_ctx, axis=-1, keepdims=True)
                 + jnp.exp(sink - m))
            acc = (_dot(p_loc.astype(jnp.bfloat16), vband[:, sel])
                   + _dot(p_ctx.astype(jnp.bfloat16), vctx[:, sel]))
            outs.append(acc / l)
        o_ref[0, :, LANES * pr:LANES * (pr + 1)] = jnp.where(low_half, outs[0], outs[1]).astype(o_ref.dtype)


def _attn_a_call(sink, qa, ka, va, n_lat):
    bsz, t, _ = qa.shape
    nb = t // TQ
    return pl.pallas_call(
        functools.partial(_attn_a_kernel, n_lat=n_lat),
        grid=(bsz, nb),
        in_specs=[pl.BlockSpec(memory_space=pltpu.SMEM),
                  pl.BlockSpec((1, TQ, 384), lambda b, i: (b, i, 0)),
                  pl.BlockSpec((1, t, 256), lambda b, i: (b, 0, 0)),
                  pl.BlockSpec((1, t, 256), lambda b, i: (b, 0, 0))],
        out_specs=pl.BlockSpec((1, TQ, 384), lambda b, i: (b, i, 0)),
        out_shape=jax.ShapeDtypeStruct((bsz, t, 384), jnp.bfloat16),
        compiler_params=pltpu.CompilerParams(dimension_semantics=("arbitrary", "arbitrary"),
                                             vmem_limit_bytes=VMEM_LIMIT),
        name="attn_a",
    )(sink, qa, ka, va)


def _flash_step(state, qm, kt, v):
    m, l, acc = state
    s = _dot(qm, kt)
    m_new = jnp.maximum(m, jnp.max(s, axis=-1, keepdims=True))
    alpha = jnp.exp(m - m_new)
    p = jnp.exp(s - m_new)
    l_new = alpha * l + jnp.sum(p, axis=-1, keepdims=True)
    acc_new = alpha * acc + _dot(p.astype(jnp.bfloat16), v)
    return m_new, l_new, acc_new


def _flash_init(n):
    one = (jnp.full((TQ, 1), NEG_INF, jnp.float32), jnp.zeros((TQ, 1), jnp.float32),
           jnp.zeros((TQ, LANES), jnp.float32))
    return tuple(one for _ in range(n))


def _attn_b_kernel(lam_ref, sub_ref, q_ref, kt_ref, v_ref, o_ref, *, n_lat, lam_init):
    i = pl.program_id(1)
    n_lat_chunks = n_lat // TK
    is_ctx = i == n_lat // TQ
    lam4 = lam_ref[...]
    lam = (jnp.exp(jnp.sum(lam4[0:1] * lam4[1:2], axis=-1, keepdims=True))
           - jnp.exp(jnp.sum(lam4[2:3] * lam4[3:4], axis=-1, keepdims=True)) + lam_init)

    lane = lax.broadcasted_iota(jnp.int32, (TQ, LANES), 1)
    n_maps = 2 * B_HEADS
    qms = []
    for j in range(n_maps):
        qg = q_ref[0, :, LANES * (j // 4):LANES * (j // 4 + 1)]
        qms.append(jnp.where((lane // B_QK_DIM) == (j % 4), qg, jnp.zeros_like(qg)))

    def chunk(c, states):
        out = []
        for g in range(2):
            kt = kt_ref[0, c, LANES * g:LANES * (g + 1), :]
            v = v_ref[0, pl.ds(pl.multiple_of(c * TK, TK), TK), LANES * g:LANES * (g + 1)]
            for jj in range(4):
                out.append(_flash_step(states[4 * g + jj], qms[4 * g + jj], kt, v))
        return tuple(out)

    states = lax.fori_loop(0, jnp.where(is_ctx, 0, n_lat_chunks), chunk, _flash_init(n_maps))
    states = chunk(n_lat_chunks, states)

    low_half = lane < B_V_DIM
    for g in range(2):
        heads = []
        for e in range(2):
            m0, l0, a0 = states[4 * g + 2 * e]
            m1, l1, a1 = states[4 * g + 2 * e + 1]
            heads.append(a0 / l0 - lam * (a1 / l1))
        o = jnp.where(low_half, heads[0], heads[1])
        sq = o * o
        ms_lo = jnp.sum(jnp.where(low_half, sq, 0.0), axis=-1, keepdims=True) * (1.0 / B_V_DIM)
        ms_hi = jnp.sum(jnp.where(low_half, 0.0, sq), axis=-1, keepdims=True) * (1.0 / B_V_DIM)
        ms = jnp.where(low_half, ms_lo, ms_hi)
        y = o * lax.rsqrt(ms + NORM_EPS) * sub_ref[...] * (1.0 - lam_init)
        o_ref[0, :, LANES * g:LANES * (g + 1)] = y.astype(o_ref.dtype)


def _attn_b_call(lam4, sub, qb, kbt, vb, n_lat, lam_init):
    bsz, t, _ = qb.shape
    nb = t // TQ
    return pl.pallas_call(
        functools.partial(_attn_b_kernel, n_lat=n_lat, lam_init=lam_init),
        grid=(bsz, nb),
        in_specs=[pl.BlockSpec((4, B_QK_DIM), lambda b, i: (0, 0)),
                  pl.BlockSpec((1, LANES), lambda b, i: (0, 0)),
                  pl.BlockSpec((1, TQ, 256), lambda b, i: (b, i, 0)),
                  pl.BlockSpec((1, nb, 256, TK), lambda b, i: (b, 0, 0, 0)),
                  pl.BlockSpec((1, t, 256), lambda b, i: (b, 0, 0))],
        out_specs=pl.BlockSpec((1, TQ, 256), lambda b, i: (b, i, 0)),
        out_shape=jax.ShapeDtypeStruct((bsz, t, 256), jnp.bfloat16),
        compiler_params=pltpu.CompilerParams(dimension_semantics=("arbitrary", "arbitrary"),
                                             vmem_limit_bytes=VMEM_LIMIT),
        name="attn_b",
    )(lam4, sub, qb, kbt, vb)


def _attn_c_kernel(q_ref, kt_ref, v_ref, o_ref, *, n_lat):
    i = pl.program_id(1)
    n_lat_chunks = n_lat // TK
    is_ctx = i == n_lat // TQ
    lane2 = lax.broadcasted_iota(jnp.int32, (TQ, 2 * LANES), 1)
    even = (lane2 < 64) | ((lane2 >= 128) & (lane2 < 160))
    odd = ((lane2 >= 64) & (lane2 < 128)) | ((lane2 >= 160) & (lane2 < 192))
    qms = []
    for h in range(C_HEADS):
        qp = q_ref[0, :, 256 * (h // 2):256 * (h // 2 + 1)]
        qms.append(jnp.where(even if h % 2 == 0 else odd, qp, jnp.zeros_like(qp)))

    def chunk(c, states):
        out = []
        for pr in range(C_HEADS // 2):
            kt = kt_ref[0, c, 256 * pr:256 * (pr + 1), :]
            v = v_ref[0, pl.ds(pl.multiple_of(c * TK, TK), TK), LANES * pr:LANES * (pr + 1)]
            for e in range(2):
                out.append(_flash_step(states[2 * pr + e], qms[2 * pr + e], kt, v))
        return tuple(out)

    states = lax.fori_loop(0, jnp.where(is_ctx, 0, n_lat_chunks), chunk, _flash_init(C_HEADS))
    states = chunk(n_lat_chunks, states)

    low_half = lax.broadcasted_iota(jnp.int32, (TQ, LANES), 1) < C_V
    for pr in range(C_HEADS // 2):
        _, l0, a0 = states[2 * pr]
        _, l1, a1 = states[2 * pr + 1]
        o_ref[0, :, LANES * pr:LANES * (pr + 1)] = jnp.where(low_half, a0 / l0, a1 / l1).astype(o_ref.dtype)


def _attn_c_call(qc, kct, vc, n_lat):
    bsz, t, _ = qc.shape
    nb = t // TQ
    return pl.pallas_call(
        functools.partial(_attn_c_kernel, n_lat=n_lat),
        grid=(bsz, nb),
        in_specs=[pl.BlockSpec((1, TQ, 768), lambda b, i: (b, i, 0)),
                  pl.BlockSpec((1, nb, 768, TK), lambda b, i: (b, 0, 0, 0)),
                  pl.BlockSpec((1, t, 384), lambda b, i: (b, 0, 0))],
        out_specs=pl.BlockSpec((1, TQ, 384), lambda b, i: (b, i, 0)),
        out_shape=jax.ShapeDtypeStruct((bsz, t, 384), jnp.bfloat16),
        compiler_params=pltpu.CompilerParams(dimension_semantics=("arbitrary", "arbitrary"),
                                             vmem_limit_bytes=VMEM_LIMIT),
        name="attn_c",
    )(qc, kct, vc)


def _post_kernel(x_ref, mod_ref, oa_ref, ob_ref, oc_ref, wout_ref, gpm_ref, gpf_ref, gqf_ref,
                 wg_ref, wu_ref, wd_ref, o_ref):
    mod = mod_ref[0]
    x = x_ref[0]
    mix = jnp.concatenate([oa_ref[0], ob_ref[0], oc_ref[0]], axis=1)
    x = x + mod[2:3] * _rms(_dot(mix, wout_ref[...]), gpm_ref[...])
    h = (_rms(x, gpf_ref[...]) * (1.0 + mod[4:5]) + mod[3:4]).astype(jnp.bfloat16)
    g = _dot(h, wg_ref[...])
    u = _dot(h, wu_ref[...])
    act = (g * jax.nn.sigmoid(g) * u).astype(jnp.bfloat16)
    ff = _dot(act, wd_ref[...])
    o_ref[0] = x + mod[5:6] * _rms(ff, gqf_ref[...])


def _post_call(xs, mods, oa, ob, oc, wout, gpm, gpf, gqf, wg, wu, wd, n_lat_blocks):
    bsz, t, d = xs.shape
    nb = t // TQ
    tok = lambda w: pl.BlockSpec((1, TQ, w), lambda i, b: (b, i, 0))
    return pl.pallas_call(
        _post_kernel,
        grid=(nb, bsz),
        in_specs=[tok(d), _mod_spec(n_lat_blocks), tok(384), tok(256), tok(384),
                  _const_spec(wout.shape), _const_spec((1, d)), _const_spec((1, d)), _const_spec((1, d)),
                  _const_spec(wg.shape), _const_spec(wu.shape), _const_spec(wd.shape)],
        out_specs=tok(d),
        out_shape=jax.ShapeDtypeStruct((bsz, t, d), jnp.float32),
        compiler_params=pltpu.CompilerParams(dimension_semantics=("arbitrary", "arbitrary"),
                                             vmem_limit_bytes=VMEM_LIMIT),
        name="post",
    )(xs, mods, oa, ob, oc, wout, gpm, gpf, gqf, wg, wu, wd)


def kernel(x, c, ctx, c_ctx, w_ada, b_ada, g_pre_mix, g_post_mix, w_in, win_sink, diff_lambda_q1, diff_lambda_k1, diff_lambda_q2, diff_lambda_k2, diff_sub_norm, mla_q_norm, mla_w_q_up, mla_kv_norm, mla_w_kv_up, w_out, g_pre_ffn, g_post_ffn, w_gate, w_up, w_down):
    bsz, n_lat, d = x.shape
    n_ctx = ctx.shape[1]
    depth = w_ada.shape[0]
    assert d == D_MODEL and n_lat % TQ == 0 and n_ctx == TQ and n_lat >= BAND
    n_lat_blocks = n_lat // TQ
    bf = jnp.bfloat16

    xs = jnp.concatenate([x, ctx], axis=1)
    cvec = jnp.zeros((8, d), jnp.float32).at[:bsz].set(c).at[bsz].set(c_ctx)
    mod_all = _ada_call(cvec, w_ada, b_ada)
    tables = _rope_tables(n_lat, n_ctx)

    ext_cols = _ext_columns()
    qup_cols = _qup_columns()
    kvup_cols = _kvup_columns()

    for l in range(depth):
        m = mod_all[l].reshape(8, 6, d)
        mods = jnp.stack([m[:bsz], jnp.broadcast_to(m[bsz], (bsz, 6, d))], axis=1).reshape(2 * bsz, 6, d)

        w_ext = jnp.concatenate([w_in[l], jnp.zeros((d, 1), jnp.float32)], axis=1)[:, ext_cols].astype(bf)
        wq = jnp.concatenate([mla_w_q_up[l], jnp.zeros((C_Q_RANK, 1), jnp.float32)], axis=1)[:, qup_cols].astype(bf)
        wkv = mla_w_kv_up[l][:, kvup_cols].astype(bf)

        qa, ka, va, qb, kbt, vb, qc, kct, vc = _proj_call(
            xs, mods, g_pre_mix[l][None], w_ext, mla_q_norm[l][None], wq, mla_kv_norm[l][None], wkv,
            tables, n_lat_blocks)

        lam_init = 0.8 - 0.6 * math.exp(-0.3 * l)
        lam4 = jnp.stack([diff_lambda_q1[l], diff_lambda_k1[l], diff_lambda_q2[l], diff_lambda_k2[l]])
        sub = jnp.concatenate([diff_sub_norm[l], diff_sub_norm[l]])[None]

        oa = _attn_a_call(win_sink[l], qa, ka, va, n_lat)
        ob = _attn_b_call(lam4, sub, qb, kbt, vb, n_lat, lam_init)
        oc = _attn_c_call(qc, kct, vc, n_lat)

        xs = _post_call(xs, mods, oa, ob, oc, w_out[l].astype(bf), g_post_mix[l][None], g_pre_ffn[l][None],
                        g_post_ffn[l][None], w_gate[l].astype(bf), w_up[l].astype(bf), w_down[l].astype(bf),
                        n_lat_blocks)
    return xs[:, :n_lat]
```

```python
import functools
import math

import numpy as np
import jax
import jax.numpy as jnp
from jax import lax
from jax.experimental import pallas as pl
from jax.experimental.pallas import tpu as pltpu

D_MODEL = 1024
GRID_W = 64
HEAD_DIM = 64
A_HEADS = 6
A_KV_HEADS = 2
A_GROUP = A_HEADS // A_KV_HEADS
WINDOW = 128
B_HEADS = 4
B_QK_DIM = 32
B_V_DIM = 64
C_HEADS = 6
C_Q_RANK = 256
C_KV_RANK = 128
C_NOPE = 64
C_ROPE = 32
C_V = 64
D_FF = 2816
ROPE_BASE = 10000.0
NORM_EPS = 1e-6
NEG_INF = -1e30
LOG2E = math.log2(math.e)
A_SCALE = HEAD_DIM ** -0.5 * LOG2E
B_SCALE = B_QK_DIM ** -0.5 * LOG2E
C_SCALE = (C_NOPE + C_ROPE) ** -0.5 * LOG2E

LANES = 128
TQ = 256
TK = 256
SUB = 4
BAND = TQ + 2 * WINDOW
VMEM_LIMIT = 56 * 1024 * 1024

_AQ, _AK, _AV = 0, 384, 512
_DQ, _DK, _DV = 640, 896, 1152
_MQ, _MKV, _MKR = 1408, 1664, 1792
_ZERO_COL = 1824

N_ROPED = 384 + 256 + 256 + 256 + 128
N_PLAIN = 256 + 256 + 256 + 128
N_EXT = 2 * N_ROPED + N_PLAIN


def _rope_partner(unit):
    half = unit // 2
    quarter = half // 2
    d = np.arange(unit)
    r = d % half
    partner = np.where(r < quarter, d + quarter, d - quarter)
    sign = np.where(r < quarter, -1.0, 1.0).astype(np.float32)
    return partner, sign


def _ext_columns():
    pa, _ = _rope_partner(HEAD_DIM)
    pb, _ = _rope_partner(B_QK_DIM)

    def rot(src, unit, partner):
        return (src // unit) * unit + partner[src % unit]

    qa = np.arange(A_HEADS * HEAD_DIM)
    dup = np.concatenate([np.arange(64), np.arange(64), 64 + np.arange(64), 64 + np.arange(64)])
    qb = np.arange(256)
    kr = np.concatenate([np.arange(32), np.arange(32)])
    zeros64 = np.full((64,), _ZERO_COL)
    main = [_AQ + qa, _AK + dup, _DQ + qb, _DK + qb, np.concatenate([_MKR + kr, zeros64])]
    rots = [_AQ + rot(qa, 64, pa), _AK + rot(dup, 64, pa), _DQ + rot(qb, 32, pb), _DK + rot(qb, 32, pb),
            np.concatenate([_MKR + rot(kr, 32, pb), zeros64])]
    plain = [_AV + dup, _DV + qb, _MQ + np.arange(256), _MKV + np.arange(128)]
    cols = np.concatenate(main + rots + plain)
    assert cols.shape[0] == N_EXT
    return cols


def _qup_columns():
    pb, _ = _rope_partner(C_ROPE)
    width = C_NOPE + C_ROPE
    zero = C_HEADS * width
    zeros64 = np.full((64,), zero)
    main, rots = [], []
    for p in range(C_HEADS // 2):
        h0, h1 = 2 * p, 2 * p + 1
        main += [h0 * width + np.arange(64), h1 * width + np.arange(64),
                 h0 * width + 64 + np.arange(32), h1 * width + 64 + np.arange(32), zeros64]
        rots += [h0 * width + 64 + pb, h1 * width + 64 + pb, zeros64]
    return np.concatenate(main + rots)


def _kvup_columns():
    width = C_NOPE + C_V
    k = np.concatenate([h * width + np.arange(64) for h in range(C_HEADS)])
    v = np.concatenate([h * width + 64 + np.arange(64) for h in range(C_HEADS)])
    return np.concatenate([k, v])


def _rope_tables(n_lat, n_ctx):
    t = jnp.arange(n_lat)
    row = (t // GRID_W).astype(jnp.float32)
    col = (t % GRID_W).astype(jnp.float32)

    def table(unit):
        half = unit // 2
        quarter = half // 2
        _, sign = _rope_partner(unit)
        lane = np.arange(LANES)
        d = lane % unit
        axis = d // half
        j = d % quarter
        freqs = ROPE_BASE ** (-jnp.arange(0, half, 2, dtype=jnp.float32) / half)
        f = freqs[j]
        pos = jnp.where(jnp.asarray(axis)[None, :] == 0, row[:, None], col[:, None])
        ang = pos * f[None, :]
        cos = jnp.cos(ang)
        sin = jnp.sin(ang) * jnp.asarray(sign[d])[None, :]
        cos = jnp.concatenate([cos, jnp.ones((n_ctx, LANES), jnp.float32)], axis=0)
        sin = jnp.concatenate([sin, jnp.zeros((n_ctx, LANES), jnp.float32)], axis=0)
        return cos, sin

    cos_a, sin_a = table(HEAD_DIM)
    cos_b, sin_b = table(B_QK_DIM)
    return cos_a, sin_a, cos_b, sin_b


def _rms(x, g):
    return x * lax.rsqrt(jnp.mean(x * x, axis=-1, keepdims=True) + NORM_EPS) * g


def _dot(a, b):
    return jnp.dot(a, b, preferred_element_type=jnp.float32)


def _dot_nt(a, b):
    return lax.dot_general(a, b, (((1,), (1,)), ((), ())), preferred_element_type=jnp.float32)


def _ada_kernel(c_ref, w_ref, b_ref, o_ref):
    c = c_ref[...]
    h = c * jax.nn.sigmoid(c)
    o_ref[0] = _dot(h, w_ref[0]) + b_ref[0]


def _ada_call(cvec, w_ada, b_ada):
    depth, d, n = w_ada.shape
    tn = 1536
    return pl.pallas_call(
        _ada_kernel,
        grid=(depth, n // tn),
        in_specs=[pl.BlockSpec((8, d), lambda l, j: (0, 0)),
                  pl.BlockSpec((1, d, tn), lambda l, j: (l, 0, j)),
                  pl.BlockSpec((1, 1, tn), lambda l, j: (l, 0, j))],
        out_specs=pl.BlockSpec((1, 8, tn), lambda l, j: (l, 0, j)),
        out_shape=jax.ShapeDtypeStruct((depth, 8, n), jnp.float32),
        compiler_params=pltpu.CompilerParams(dimension_semantics=("arbitrary", "arbitrary"),
                                             vmem_limit_bytes=VMEM_LIMIT),
        name="adaln",
    )(cvec, w_ada, b_ada.reshape(depth, 1, n))


def _proj_kernel(x_ref, mod_ref, gpre_ref, w_ref, qn_ref, wq_ref, kvn_ref, wkv_ref,
                 ca_ref, sa_ref, cb_ref, sb_ref,
                 qa_ref, ka_ref, va_ref, qb_ref, kbt_ref, vb_ref, qc_ref, kct_ref, vc_ref):
    mod = mod_ref[0]
    x = x_ref[0]
    h = _rms(x, gpre_ref[...]) * (1.0 + mod[1:2]) + mod[0:1]
    p = _dot(h.astype(jnp.bfloat16), w_ref[...])

    ca, sa, cb, sb = ca_ref[...], sa_ref[...], cb_ref[...], sb_ref[...]

    def roped(off, width, cos, sin):
        n = width // LANES
        main = p[:, off:off + width]
        rot = p[:, N_ROPED + off:N_ROPED + off + width]
        return main * jnp.concatenate([cos] * n, axis=1) + rot * jnp.concatenate([sin] * n, axis=1)

    qa_ref[0] = (roped(0, 384, ca, sa) * A_SCALE).astype(jnp.bfloat16)
    ka_ref[0] = roped(384, 256, ca, sa).astype(jnp.bfloat16)
    qb_ref[0] = (roped(640, 256, cb, sb) * B_SCALE).astype(jnp.bfloat16)
    kbt_ref[0, 0] = roped(896, 256, cb, sb).T.astype(jnp.bfloat16)
    kr = roped(1152, 128, cb, sb)

    o = 2 * N_ROPED
    va_ref[0] = p[:, o:o + 256].astype(jnp.bfloat16)
    vb_ref[0] = p[:, o + 256:o + 512].astype(jnp.bfloat16)
    cq = p[:, o + 512:o + 768]
    ckv = p[:, o + 768:o + 896]

    qc = _dot(_rms(cq, qn_ref[...]).astype(jnp.bfloat16), wq_ref[...])
    kv = _dot(_rms(ckv, kvn_ref[...]).astype(jnp.bfloat16), wkv_ref[...])
    vc_ref[0] = kv[:, 384:].astype(jnp.bfloat16)
    kr_t = kr.T.astype(jnp.bfloat16)
    for pr in range(C_HEADS // 2):
        nope = qc[:, 256 * pr:256 * pr + 128]
        rope = (qc[:, 256 * pr + 128:256 * pr + 256] * cb
                + qc[:, 768 + 128 * pr:768 + 128 * pr + 128] * sb)
        qc_ref[0, :, 256 * pr:256 * pr + 128] = (nope * C_SCALE).astype(jnp.bfloat16)
        qc_ref[0, :, 256 * pr + 128:256 * pr + 256] = (rope * C_SCALE).astype(jnp.bfloat16)
        kct_ref[0, 0, 256 * pr:256 * pr + 128, :] = kv[:, 128 * pr:128 * pr + 128].T.astype(jnp.bfloat16)
        kct_ref[0, 0, 256 * pr + 128:256 * pr + 256, :] = kr_t


def _const_spec(shape):
    nd = len(shape)
    return pl.BlockSpec(shape, lambda *_: (0,) * nd, pipeline_mode=pl.Buffered(1))


def _mod_spec(n_lat_blocks):
    return pl.BlockSpec((1, 6, D_MODEL), lambda i, b: (2 * b + (i >= n_lat_blocks).astype(jnp.int32), 0, 0))


def _proj_call(xs, mods, gpre, w_ext, qn, wq, kvn, wkv, tables, n_lat_blocks):
    bsz, t, d = xs.shape
    nb = t // TQ
    tok = lambda w: pl.BlockSpec((1, TQ, w), lambda i, b: (b, i, 0))
    tbl = pl.BlockSpec((TQ, LANES), lambda i, b: (i, 0))
    kt = lambda r: pl.BlockSpec((1, 1, r, TQ), lambda i, b: (b, i, 0, 0))
    bf = jnp.bfloat16
    out_shape = [jax.ShapeDtypeStruct((bsz, t, 384), bf), jax.ShapeDtypeStruct((bsz, t, 256), bf),
                 jax.ShapeDtypeStruct((bsz, t, 256), bf), jax.ShapeDtypeStruct((bsz, t, 256), bf),
                 jax.ShapeDtypeStruct((bsz, nb, 256, TQ), bf), jax.ShapeDtypeStruct((bsz, t, 256), bf),
                 jax.ShapeDtypeStruct((bsz, t, 768), bf), jax.ShapeDtypeStruct((bsz, nb, 768, TQ), bf),
                 jax.ShapeDtypeStruct((bsz, t, 384), bf)]
    out_specs = [tok(384), tok(256), tok(256), tok(256), kt(256), tok(256), tok(768), kt(768), tok(384)]
    return pl.pallas_call(
        _proj_kernel,
        grid=(nb, bsz),
        in_specs=[tok(d), _mod_spec(n_lat_blocks), _const_spec((1, d)), _const_spec(w_ext.shape),
                  _const_spec(qn.shape), _const_spec(wq.shape), _const_spec(kvn.shape), _const_spec(wkv.shape),
                  tbl, tbl, tbl, tbl],
        out_specs=out_specs,
        out_shape=out_shape,
        compiler_params=pltpu.CompilerParams(dimension_semantics=("arbitrary", "arbitrary"),
                                             vmem_limit_bytes=VMEM_LIMIT),
        name="proj",
    )(xs, mods, gpre, w_ext, qn, wq, kvn, wkv, *tables)


def _attn_a_kernel(sink_ref, q_ref, k_ref, v_ref, o_ref, *, n_lat):
    i = pl.program_id(1)
    is_ctx = i == n_lat // TQ
    t0 = i * TQ
    start = pl.multiple_of(jnp.clip(t0 - WINDOW, 0, n_lat - BAND), WINDOW)
    kband = k_ref[0, pl.ds(start, BAND), :]
    vband = v_ref[0, pl.ds(start, BAND), :]
    kctx = k_ref[0, n_lat:, :]
    vctx = v_ref[0, n_lat:, :]
    qpos = t0 + jnp.where(is_ctx, 4 * n_lat, 0) + lax.broadcasted_iota(jnp.int32, (TQ, BAND), 0)
    kpos = start + lax.broadcasted_iota(jnp.int32, (TQ, BAND), 1)
    mask = jnp.abs(qpos - kpos) <= WINDOW
    low_half = lax.broadcasted_iota(jnp.int32, (TQ, LANES), 1) < HEAD_DIM

    for pr in range(A_HEADS // 2):
        qg = q_ref[0, :, LANES * pr:LANES * (pr + 1)]
        outs = []
        for e in range(2):
            head = 2 * pr + e
            kv = head // A_GROUP
            sel = slice(LANES * kv, LANES * (kv + 1))
            qm = jnp.where(low_half if e == 0 else jnp.logical_not(low_half), qg, jnp.zeros_like(qg))
            s_loc = jnp.where(mask, _dot_nt(qm, kband[:, sel]), NEG_INF)
            s_ctx = _dot_nt(qm, kctx[:, sel])
            sink = sink_ref[head] * LOG2E
            m = jnp.maximum(jnp.maximum(jnp.max(s_loc, axis=-1, keepdims=True),
                                        jnp.max(s_ctx, axis=-1, keepdims=True)), sink)
            p_loc = jnp.exp2(s_loc - m)
            p_ctx = jnp.exp2(s_ctx - m)
            l = (jnp.sum(p_loc, axis=-1, keepdims=True) + jnp.sum(p_ctx, axis=-1, keepdims=True)
                 + jnp.exp2(sink - m))
            acc = (_dot(p_loc.astype(jnp.bfloat16), vband[:, sel])
                   + _dot(p_ctx.astype(jnp.bfloat16), vctx[:, sel]))
            outs.append(acc / l)
        o_ref[0, :, LANES * pr:LANES * (pr + 1)] = jnp.where(low_half, outs[0], outs[1]).astype(o_ref.dtype)


def _attn_a_call(sink, qa, ka, va, n_lat):
    bsz, t, _ = qa.shape
    nb = t // TQ
    return pl.pallas_call(
        functools.partial(_attn_a_kernel, n_lat=n_lat),
        grid=(bsz, nb),
        in_specs=[pl.BlockSpec(memory_space=pltpu.SMEM),
                  pl.BlockSpec((1, TQ, 384), lambda b, i: (b, i, 0)),
                  pl.BlockSpec((1, t, 256), lambda b, i: (b, 0, 0)),
                  pl.BlockSpec((1, t, 256), lambda b, i: (b, 0, 0))],
        out_specs=pl.BlockSpec((1, TQ, 384), lambda b, i: (b, i, 0)),
        out_shape=jax.ShapeDtypeStruct((bsz, t, 384), jnp.bfloat16),
        compiler_params=pltpu.CompilerParams(dimension_semantics=("arbitrary", "arbitrary"),
                                             vmem_limit_bytes=VMEM_LIMIT),
        name="attn_a",
    )(sink, qa, ka, va)


def _flash_init(m_ref, l_ref, acc_ref):
    m_ref[...] = jnp.full(m_ref.shape, NEG_INF, jnp.float32)
    l_ref[...] = jnp.zeros(l_ref.shape, jnp.float32)
    acc_ref[...] = jnp.zeros(acc_ref.shape, jnp.float32)


def _flash_scores(qm, kts):
    parts = []
    for kt in kts:
        s = _dot(qm, kt)
        parts += [s[:, k * LANES:(k + 1) * LANES] for k in range(TK // LANES)]
    return parts


def _flash_update(j, m_ref, l_ref, acc_ref, parts, v):
    m_old = m_ref[j]
    m_new = jnp.maximum(m_old, jnp.max(functools.reduce(jnp.maximum, parts), axis=1, keepdims=True))
    alpha = jnp.exp2(m_old - m_new)
    ps = [jnp.exp2(part - m_new) for part in parts]
    l_ref[j] = alpha * l_ref[j] + functools.reduce(jnp.add, ps)
    pb = jnp.concatenate([p.astype(jnp.bfloat16) for p in ps], axis=1)
    acc_ref[j] = alpha * acc_ref[j] + _dot(pb, v)
    m_ref[j] = m_new


def _flash_maps(m_ref, l_ref, acc_ref, qms, kts_of, v_of):
    n = len(qms)
    nxt = _flash_scores(qms[0], kts_of(0))
    for j in range(n):
        parts = nxt
        if j + 1 < n:
            nxt = _flash_scores(qms[j + 1], kts_of(j + 1))
        _flash_update(j, m_ref, l_ref, acc_ref, parts, v_of(j))


def _flash_result(j, l_ref, acc_ref):
    return acc_ref[j] / jnp.sum(l_ref[j], axis=1, keepdims=True)


def _flash_scratch(n_maps):
    return [pltpu.VMEM((n_maps, TQ, LANES), jnp.float32) for _ in range(3)]


def _attn_b_kernel(lam_ref, sub_ref, q_ref, kt_ref, v_ref, o_ref, m_ref, l_ref, acc_ref, *, n_lat, lam_init):
    i = pl.program_id(1)
    n_lat_chunks = n_lat // TK
    is_ctx = i == n_lat // TQ
    lam4 = lam_ref[...]
    lam = (jnp.exp(jnp.sum(lam4[0:1] * lam4[1:2], axis=-1, keepdims=True))
           - jnp.exp(jnp.sum(lam4[2:3] * lam4[3:4], axis=-1, keepdims=True)) + lam_init)

    lane = lax.broadcasted_iota(jnp.int32, (TQ, LANES), 1)
    n_maps = 2 * B_HEADS
    qms = []
    for j in range(n_maps):
        qg = q_ref[0, :, LANES * (j // 4):LANES * (j // 4 + 1)]
        qms.append(jnp.where((lane // B_QK_DIM) == (j % 4), qg, jnp.zeros_like(qg)))

    _flash_init(m_ref, l_ref, acc_ref)

    def step(c0, n_sub):
        def kts_of(j):
            g = j // 4
            return [kt_ref[0, c0 + k, LANES * g:LANES * (g + 1), :] for k in range(n_sub)]

        def v_of(j):
            g = j // 4
            return v_ref[0, pl.ds(pl.multiple_of(c0 * TK, TK), n_sub * TK), LANES * g:LANES * (g + 1)]

        _flash_maps(m_ref, l_ref, acc_ref, qms, kts_of, v_of)

    @pl.loop(0, jnp.where(is_ctx, 0, n_lat_chunks // SUB))
    def _(c):
        step(c * SUB, SUB)

    step(n_lat_chunks, 1)

    low_half = lane < B_V_DIM
    for g in range(2):
        heads = []
        for e in range(2):
            heads.append(_flash_result(4 * g + 2 * e, l_ref, acc_ref)
                         - lam * _flash_result(4 * g + 2 * e + 1, l_ref, acc_ref))
        o = jnp.where(low_half, heads[0], heads[1])
        sq = o * o
        ms_lo = jnp.sum(jnp.where(low_half, sq, 0.0), axis=-1, keepdims=True) * (1.0 / B_V_DIM)
        ms_hi = jnp.sum(jnp.where(low_half, 0.0, sq), axis=-1, keepdims=True) * (1.0 / B_V_DIM)
        ms = jnp.where(low_half, ms_lo, ms_hi)
        y = o * lax.rsqrt(ms + NORM_EPS) * sub_ref[...] * (1.0 - lam_init)
        o_ref[0, :, LANES * g:LANES * (g + 1)] = y.astype(o_ref.dtype)


def _attn_b_call(lam4, sub, qb, kbt, vb, n_lat, lam_init):
    bsz, t, _ = qb.shape
    nb = t // TQ
    return pl.pallas_call(
        functools.partial(_attn_b_kernel, n_lat=n_lat, lam_init=lam_init),
        grid=(bsz, nb),
        in_specs=[pl.BlockSpec((4, B_QK_DIM), lambda b, i: (0, 0)),
                  pl.BlockSpec((1, LANES), lambda b, i: (0, 0)),
                  pl.BlockSpec((1, TQ, 256), lambda b, i: (b, i, 0)),
                  pl.BlockSpec((1, nb, 256, TK), lambda b, i: (b, 0, 0, 0)),
                  pl.BlockSpec((1, t, 256), lambda b, i: (b, 0, 0))],
        out_specs=pl.BlockSpec((1, TQ, 256), lambda b, i: (b, i, 0)),
        out_shape=jax.ShapeDtypeStruct((bsz, t, 256), jnp.bfloat16),
        scratch_shapes=_flash_scratch(2 * B_HEADS),
        compiler_params=pltpu.CompilerParams(dimension_semantics=("arbitrary", "arbitrary"),
                                             vmem_limit_bytes=VMEM_LIMIT),
        name="attn_b",
    )(lam4, sub, qb, kbt, vb)


def _attn_c_kernel(q_ref, kt_ref, v_ref, o_ref, m_ref, l_ref, acc_ref, *, n_lat):
    i = pl.program_id(1)
    n_lat_chunks = n_lat // TK
    is_ctx = i == n_lat // TQ
    lane2 = lax.broadcasted_iota(jnp.int32, (TQ, 2 * LANES), 1)
    even = (lane2 < 64) | ((lane2 >= 128) & (lane2 < 160))
    odd = ((lane2 >= 64) & (lane2 < 128)) | ((lane2 >= 160) & (lane2 < 192))
    qms = []
    for h in range(C_HEADS):
        qp = q_ref[0, :, 256 * (h // 2):256 * (h // 2 + 1)]
        qms.append(jnp.where(even if h % 2 == 0 else odd, qp, jnp.zeros_like(qp)))

    _flash_init(m_ref, l_ref, acc_ref)

    def step(c0, n_sub):
        def kts_of(h):
            pr = h // 2
            return [kt_ref[0, c0 + k, 256 * pr:256 * (pr + 1), :] for k in range(n_sub)]

        def v_of(h):
            pr = h // 2
            return v_ref[0, pl.ds(pl.multiple_of(c0 * TK, TK), n_sub * TK), LANES * pr:LANES * (pr + 1)]

        _flash_maps(m_ref, l_ref, acc_ref, qms, kts_of, v_of)

    @pl.loop(0, jnp.where(is_ctx, 0, n_lat_chunks // SUB))
    def _(c):
        step(c * SUB, SUB)

    step(n_lat_chunks, 1)

    low_half = lax.broadcasted_iota(jnp.int32, (TQ, LANES), 1) < C_V
    for pr in range(C_HEADS // 2):
        o = jnp.where(low_half, _flash_result(2 * pr, l_ref, acc_ref), _flash_result(2 * pr + 1, l_ref, acc_ref))
        o_ref[0, :, LANES * pr:LANES * (pr + 1)] = o.astype(o_ref.dtype)


def _attn_c_call(qc, kct, vc, n_lat):
    bsz, t, _ = qc.shape
    nb = t // TQ
    return pl.pallas_call(
        functools.partial(_attn_c_kernel, n_lat=n_lat),
        grid=(bsz, nb),
        in_specs=[pl.BlockSpec((1, TQ, 768), lambda b, i: (b, i, 0)),
                  pl.BlockSpec((1, nb, 768, TK), lambda b, i: (b, 0, 0, 0)),
                  pl.BlockSpec((1, t, 384), lambda b, i: (b, 0, 0))],
        out_specs=pl.BlockSpec((1, TQ, 384), lambda b, i: (b, i, 0)),
        out_shape=jax.ShapeDtypeStruct((bsz, t, 384), jnp.bfloat16),
        scratch_shapes=_flash_scratch(C_HEADS),
        compiler_params=pltpu.CompilerParams(dimension_semantics=("arbitrary", "arbitrary"),
                                             vmem_limit_bytes=VMEM_LIMIT),
        name="attn_c",
    )(qc, kct, vc)


def _post_kernel(x_ref, mod_ref, oa_ref, ob_ref, oc_ref, wout_ref, gpm_ref, gpf_ref, gqf_ref,
                 wg_ref, wu_ref, wd_ref, o_ref):
    mod = mod_ref[0]
    x = x_ref[0]
    mix = jnp.concatenate([oa_ref[0], ob_ref[0], oc_ref[0]], axis=1)
    x = x + mod[2:3] * _rms(_dot(mix, wout_ref[...]), gpm_ref[...])
    h = (_rms(x, gpf_ref[...]) * (1.0 + mod[4:5]) + mod[3:4]).astype(jnp.bfloat16)
    g = _dot(h, wg_ref[...])
    u = _dot(h, wu_ref[...])
    act = (g * jax.nn.sigmoid(g) * u).astype(jnp.bfloat16)
    ff = _dot(act, wd_ref[...])
    o_ref[0] = x + mod[5:6] * _rms(ff, gqf_ref[...])


def _post_call(xs, mods, oa, ob, oc, wout, gpm, gpf, gqf, wg, wu, wd, n_lat_blocks):
    bsz, t, d = xs.shape
    nb = t // TQ
    tok = lambda w: pl.BlockSpec((1, TQ, w), lambda i, b: (b, i, 0))
    return pl.pallas_call(
        _post_kernel,
        grid=(nb, bsz),
        in_specs=[tok(d), _mod_spec(n_lat_blocks), tok(384), tok(256), tok(384),
                  _const_spec(wout.shape), _const_spec((1, d)), _const_spec((1, d)), _const_spec((1, d)),
                  _const_spec(wg.shape), _const_spec(wu.shape), _const_spec(wd.shape)],
        out_specs=tok(d),
        out_shape=jax.ShapeDtypeStruct((bsz, t, d), jnp.float32),
        compiler_params=pltpu.CompilerParams(dimension_semantics=("arbitrary", "arbitrary"),
                                             vmem_limit_bytes=VMEM_LIMIT),
        name="post",
    )(xs, mods, oa, ob, oc, wout, gpm, gpf, gqf, wg, wu, wd)


def kernel(x, c, ctx, c_ctx, w_ada, b_ada, g_pre_mix, g_post_mix, w_in, win_sink, diff_lambda_q1, diff_lambda_k1, diff_lambda_q2, diff_lambda_k2, diff_sub_norm, mla_q_norm, mla_w_q_up, mla_kv_norm, mla_w_kv_up, w_out, g_pre_ffn, g_post_ffn, w_gate, w_up, w_down):
    bsz, n_lat, d = x.shape
    n_ctx = ctx.shape[1]
    depth = w_ada.shape[0]
    assert d == D_MODEL and n_lat % (SUB * TK) == 0 and n_ctx == TQ and n_lat >= BAND
    n_lat_blocks = n_lat // TQ
    bf = jnp.bfloat16

    xs = jnp.concatenate([x, ctx], axis=1)
    cvec = jnp.zeros((8, d), jnp.float32).at[:bsz].set(c).at[bsz].set(c_ctx)
    mod_all = _ada_call(cvec, w_ada, b_ada)
    tables = _rope_tables(n_lat, n_ctx)

    ext_cols = _ext_columns()
    qup_cols = _qup_columns()
    kvup_cols = _kvup_columns()

    for l in range(depth):
        m = mod_all[l].reshape(8, 6, d)
        mods = jnp.stack([m[:bsz], jnp.broadcast_to(m[bsz], (bsz, 6, d))], axis=1).reshape(2 * bsz, 6, d)

        w_ext = jnp.concatenate([w_in[l], jnp.zeros((d, 1), jnp.float32)], axis=1)[:, ext_cols].astype(bf)
        wq = jnp.concatenate([mla_w_q_up[l], jnp.zeros((C_Q_RANK, 1), jnp.float32)], axis=1)[:, qup_cols].astype(bf)
        wkv = mla_w_kv_up[l][:, kvup_cols].astype(bf)

        qa, ka, va, qb, kbt, vb, qc, kct, vc = _proj_call(
            xs, mods, g_pre_mix[l][None], w_ext, mla_q_norm[l][None], wq, mla_kv_norm[l][None], wkv,
            tables, n_lat_blocks)

        lam_init = 0.8 - 0.6 * math.exp(-0.3 * l)
        lam4 = jnp.stack([diff_lambda_q1[l], diff_lambda_k1[l], diff_lambda_q2[l], diff_lambda_k2[l]])
        sub = jnp.concatenate([diff_sub_norm[l], diff_sub_norm[l]])[None]

        oa = _attn_a_call(win_sink[l], qa, ka, va, n_lat)
        ob = _attn_b_call(lam4, sub, qb, kbt, vb, n_lat, lam_init)
        oc = _attn_c_call(qc, kct, vc, n_lat)

        xs = _post_call(xs, mods, oa, ob, oc, w_out[l].astype(bf), g_post_mix[l][None], g_pre_ffn[l][None],
                        g_post_ffn[l][None], w_gate[l].astype(bf), w_up[l].astype(bf), w_down[l].astype(bf),
                        n_lat_blocks)
    return xs[:, :n_lat]
```

```python
import functools
import math

import numpy as np
import jax
import jax.numpy as jnp
from jax import lax
from jax.experimental import pallas as pl
from jax.experimental.pallas import tpu as pltpu

D_MODEL = 1024
GRID_W = 64
HEAD_DIM = 64
A_HEADS = 6
A_KV_HEADS = 2
A_GROUP = A_HEADS // A_KV_HEADS
WINDOW = 128
B_HEADS = 4
B_QK_DIM = 32
B_V_DIM = 64
C_HEADS = 6
C_Q_RANK = 256
C_KV_RANK = 128
C_NOPE = 64
C_ROPE = 32
C_V = 64
D_FF = 2816
ROPE_BASE = 10000.0
NORM_EPS = 1e-6
NEG_INF = -1e30
LOG2E = math.log2(math.e)
A_SCALE = HEAD_DIM ** -0.5 * LOG2E
B_SCALE = B_QK_DIM ** -0.5 * LOG2E
C_SCALE = (C_NOPE + C_ROPE) ** -0.5 * LOG2E

LANES = 128
TQ = 256
TM_POST = 512
TK = 256
SUB = 11
LOOKAHEAD = 1
BAND = TQ + 2 * WINDOW
VMEM_LIMIT = 56 * 1024 * 1024

_AQ, _AK, _AV = 0, 384, 512
_DQ, _DK, _DV = 640, 896, 1152
_MQ, _MKV, _MKR = 1408, 1664, 1792
_ZERO_COL = 1824

N_ROPED = 384 + 256 + 256 + 256 + 128
N_PLAIN = 256 + 256 + 256 + 128
N_EXT = 2 * N_ROPED + N_PLAIN


def _rope_partner(unit):
    half = unit // 2
    quarter = half // 2
    d = np.arange(unit)
    r = d % half
    partner = np.where(r < quarter, d + quarter, d - quarter)
    sign = np.where(r < quarter, -1.0, 1.0).astype(np.float32)
    return partner, sign


def _ext_columns():
    pa, _ = _rope_partner(HEAD_DIM)
    pb, _ = _rope_partner(B_QK_DIM)

    def rot(src, unit, partner):
        return (src // unit) * unit + partner[src % unit]

    qa = np.arange(A_HEADS * HEAD_DIM)
    dup = np.concatenate([np.arange(64), np.arange(64), 64 + np.arange(64), 64 + np.arange(64)])
    qb = np.arange(256)
    kr = np.concatenate([np.arange(32), np.arange(32)])
    zeros64 = np.full((64,), _ZERO_COL)
    main = [_AQ + qa, _AK + dup, _DQ + qb, _DK + qb, np.concatenate([_MKR + kr, zeros64])]
    rots = [_AQ + rot(qa, 64, pa), _AK + rot(dup, 64, pa), _DQ + rot(qb, 32, pb), _DK + rot(qb, 32, pb),
            np.concatenate([_MKR + rot(kr, 32, pb), zeros64])]
    plain = [_AV + dup, _DV + qb, _MQ + np.arange(256), _MKV + np.arange(128)]
    cols = np.concatenate(main + rots + plain)
    assert cols.shape[0] == N_EXT
    return cols


def _qup_columns():
    pb, _ = _rope_partner(C_ROPE)
    width = C_NOPE + C_ROPE
    zero = C_HEADS * width
    zeros64 = np.full((64,), zero)
    main, rots = [], []
    for p in range(C_HEADS // 2):
        h0, h1 = 2 * p, 2 * p + 1
        main += [h0 * width + np.arange(64), h1 * width + np.arange(64),
                 h0 * width + 64 + np.arange(32), h1 * width + 64 + np.arange(32), zeros64]
        rots += [h0 * width + 64 + pb, h1 * width + 64 + pb, zeros64]
    return np.concatenate(main + rots)


def _kvup_columns():
    width = C_NOPE + C_V
    k = np.concatenate([h * width + np.arange(64) for h in range(C_HEADS)])
    v = np.concatenate([h * width + 64 + np.arange(64) for h in range(C_HEADS)])
    return np.concatenate([k, v])


def _rope_tables(n_lat, n_ctx):
    t = jnp.arange(n_lat)
    row = (t // GRID_W).astype(jnp.float32)
    col = (t % GRID_W).astype(jnp.float32)

    def table(unit):
        half = unit // 2
        quarter = half // 2
        _, sign = _rope_partner(unit)
        lane = np.arange(LANES)
        d = lane % unit
        axis = d // half
        j = d % quarter
        freqs = ROPE_BASE ** (-jnp.arange(0, half, 2, dtype=jnp.float32) / half)
        f = freqs[j]
        pos = jnp.where(jnp.asarray(axis)[None, :] == 0, row[:, None], col[:, None])
        ang = pos * f[None, :]
        cos = jnp.cos(ang)
        sin = jnp.sin(ang) * jnp.asarray(sign[d])[None, :]
        cos = jnp.concatenate([cos, jnp.ones((n_ctx, LANES), jnp.float32)], axis=0)
        sin = jnp.concatenate([sin, jnp.zeros((n_ctx, LANES), jnp.float32)], axis=0)
        return cos, sin

    cos_a, sin_a = table(HEAD_DIM)
    cos_b, sin_b = table(B_QK_DIM)
    return cos_a, sin_a, cos_b, sin_b


def _rms(x, g):
    return x * lax.rsqrt(jnp.mean(x * x, axis=-1, keepdims=True) + NORM_EPS) * g


def _dot(a, b):
    return jnp.dot(a, b, preferred_element_type=jnp.float32)


def _dot_nt(a, b):
    return lax.dot_general(a, b, (((1,), (1,)), ((), ())), preferred_element_type=jnp.float32)


def _ada_kernel(c_ref, w_ref, b_ref, o_ref):
    c = c_ref[...]
    h = c * jax.nn.sigmoid(c)
    o_ref[0] = _dot(h, w_ref[0]) + b_ref[0]


def _ada_call(cvec, w_ada, b_ada):
    depth, d, n = w_ada.shape
    tn = 1536
    return pl.pallas_call(
        _ada_kernel,
        grid=(depth, n // tn),
        in_specs=[pl.BlockSpec((8, d), lambda l, j: (0, 0)),
                  pl.BlockSpec((1, d, tn), lambda l, j: (l, 0, j)),
                  pl.BlockSpec((1, 1, tn), lambda l, j: (l, 0, j))],
        out_specs=pl.BlockSpec((1, 8, tn), lambda l, j: (l, 0, j)),
        out_shape=jax.ShapeDtypeStruct((depth, 8, n), jnp.float32),
        compiler_params=pltpu.CompilerParams(dimension_semantics=("arbitrary", "arbitrary"),
                                             vmem_limit_bytes=VMEM_LIMIT),
        name="adaln",
    )(cvec, w_ada, b_ada.reshape(depth, 1, n))


def _proj_kernel(x_ref, ctx_ref, mod_ref, gpre_ref, w_ref, qn_ref, wq_ref, kvn_ref, wkv_ref,
                 ca_ref, sa_ref, cb_ref, sb_ref,
                 qa_ref, ka_ref, va_ref, qb_ref, kbt_ref, vb_ref, qc_ref, kct_ref, vc_ref, *, n_lat_blocks):
    mod = mod_ref[0]
    x = jnp.where(pl.program_id(1) == n_lat_blocks, ctx_ref[0], x_ref[0])
    h = _rms(x, gpre_ref[...]) * (1.0 + mod[1:2]) + mod[0:1]
    p = _dot(h.astype(jnp.bfloat16), w_ref[...])

    ca, sa, cb, sb = ca_ref[...], sa_ref[...], cb_ref[...], sb_ref[...]

    def roped(off, width, cos, sin):
        n = width // LANES
        main = p[:, off:off + width]
        rot = p[:, N_ROPED + off:N_ROPED + off + width]
        return main * jnp.concatenate([cos] * n, axis=1) + rot * jnp.concatenate([sin] * n, axis=1)

    qa_ref[0] = (roped(0, 384, ca, sa) * A_SCALE).astype(jnp.bfloat16)
    ka_ref[0] = roped(384, 256, ca, sa).astype(jnp.bfloat16)
    qb_ref[0] = (roped(640, 256, cb, sb) * B_SCALE).astype(jnp.bfloat16)
    kbt_ref[0, 0] = roped(896, 256, cb, sb).T.astype(jnp.bfloat16)
    kr = roped(1152, 128, cb, sb)

    o = 2 * N_ROPED
    va_ref[0] = p[:, o:o + 256].astype(jnp.bfloat16)
    vb_ref[0] = p[:, o + 256:o + 512].astype(jnp.bfloat16)
    cq = p[:, o + 512:o + 768]
    ckv = p[:, o + 768:o + 896]

    qc = _dot(_rms(cq, qn_ref[...]).astype(jnp.bfloat16), wq_ref[...])
    kv = _dot(_rms(ckv, kvn_ref[...]).astype(jnp.bfloat16), wkv_ref[...])
    vc_ref[0] = kv[:, 384:].astype(jnp.bfloat16)
    kr_t = kr.T.astype(jnp.bfloat16)
    for pr in range(C_HEADS // 2):
        nope = qc[:, 256 * pr:256 * pr + 128]
        rope = (qc[:, 256 * pr + 128:256 * pr + 256] * cb
                + qc[:, 768 + 128 * pr:768 + 128 * pr + 128] * sb)
        qc_ref[0, :, 256 * pr:256 * pr + 128] = (nope * C_SCALE).astype(jnp.bfloat16)
        qc_ref[0, :, 256 * pr + 128:256 * pr + 256] = (rope * C_SCALE).astype(jnp.bfloat16)
        kct_ref[0, 0, 256 * pr:256 * pr + 128, :] = kv[:, 128 * pr:128 * pr + 128].T.astype(jnp.bfloat16)
        kct_ref[0, 0, 256 * pr + 128:256 * pr + 256, :] = kr_t


def _const_spec(shape):
    nd = len(shape)
    return pl.BlockSpec(shape, lambda *_: (0,) * nd, pipeline_mode=pl.Buffered(1))


def _proj_call(x, ctx, mods, gpre, w_ext, qn, wq, kvn, wkv, tables):
    bsz, n_lat, d = x.shape
    n_lat_blocks = n_lat // TQ
    nb = n_lat_blocks + 1
    t = nb * TQ
    tok = lambda w: pl.BlockSpec((1, TQ, w), lambda b, i: (b, i, 0))
    tbl = pl.BlockSpec((TQ, LANES), lambda b, i: (i, 0))
    kt = lambda r: pl.BlockSpec((1, 1, r, TQ), lambda b, i: (b, i, 0, 0))
    x_spec = pl.BlockSpec((1, TQ, d), lambda b, i: (b, jnp.minimum(i, n_lat_blocks - 1), 0))
    ctx_spec = pl.BlockSpec((1, TQ, d), lambda b, i: (b, 0, 0))
    mod_spec = pl.BlockSpec((1, 6, d), lambda b, i: (2 * b + (i >= n_lat_blocks).astype(jnp.int32), 0, 0))
    bf = jnp.bfloat16
    out_shape = [jax.ShapeDtypeStruct((bsz, t, 384), bf), jax.ShapeDtypeStruct((bsz, t, 256), bf),
                 jax.ShapeDtypeStruct((bsz, t, 256), bf), jax.ShapeDtypeStruct((bsz, t, 256), bf),
                 jax.ShapeDtypeStruct((bsz, nb, 256, TQ), bf), jax.ShapeDtypeStruct((bsz, t, 256), bf),
                 jax.ShapeDtypeStruct((bsz, t, 768), bf), jax.ShapeDtypeStruct((bsz, nb, 768, TQ), bf),
                 jax.ShapeDtypeStruct((bsz, t, 384), bf)]
    out_specs = [tok(384), tok(256), tok(256), tok(256), kt(256), tok(256), tok(768), kt(768), tok(384)]
    return pl.pallas_call(
        functools.partial(_proj_kernel, n_lat_blocks=n_lat_blocks),
        grid=(bsz, nb),
        in_specs=[x_spec, ctx_spec, mod_spec, _const_spec((1, d)), _const_spec(w_ext.shape),
                  _const_spec(qn.shape), _const_spec(wq.shape), _const_spec(kvn.shape), _const_spec(wkv.shape),
                  tbl, tbl, tbl, tbl],
        out_specs=out_specs,
        out_shape=out_shape,
        compiler_params=pltpu.CompilerParams(dimension_semantics=("arbitrary", "arbitrary"),
                                             vmem_limit_bytes=VMEM_LIMIT),
        name="proj",
    )(x, ctx, mods, gpre, w_ext, qn, wq, kvn, wkv, *tables)


def _attn_a_kernel(sink_ref, q_ref, k_ref, v_ref, o_ref, *, n_lat):
    i = pl.program_id(1)
    is_ctx = i == n_lat // TQ
    t0 = i * TQ
    start = pl.multiple_of(jnp.clip(t0 - WINDOW, 0, n_lat - BAND), WINDOW)
    kband = k_ref[0, pl.ds(start, BAND), :]
    vband = v_ref[0, pl.ds(start, BAND), :]
    kctx = k_ref[0, n_lat:, :]
    vctx = v_ref[0, n_lat:, :]
    qpos = t0 + jnp.where(is_ctx, 4 * n_lat, 0) + lax.broadcasted_iota(jnp.int32, (TQ, BAND), 0)
    kpos = start + lax.broadcasted_iota(jnp.int32, (TQ, BAND), 1)
    mask = jnp.abs(qpos - kpos) <= WINDOW
    low_half = lax.broadcasted_iota(jnp.int32, (TQ, LANES), 1) < HEAD_DIM

    for pr in range(A_HEADS // 2):
        qg = q_ref[0, :, LANES * pr:LANES * (pr + 1)]
        outs = []
        for e in range(2):
            head = 2 * pr + e
            kv = head // A_GROUP
            sel = slice(LANES * kv, LANES * (kv + 1))
            qm = jnp.where(low_half if e == 0 else jnp.logical_not(low_half), qg, jnp.zeros_like(qg))
            s_loc = jnp.where(mask, _dot_nt(qm, kband[:, sel]), NEG_INF)
            s_ctx = _dot_nt(qm, kctx[:, sel])
            sink = sink_ref[head] * LOG2E
            m = jnp.maximum(jnp.maximum(jnp.max(s_loc, axis=-1, keepdims=True),
                                        jnp.max(s_ctx, axis=-1, keepdims=True)), sink)
            p_loc = jnp.exp2(s_loc - m)
            p_ctx = jnp.exp2(s_ctx - m)
            l = (jnp.sum(p_loc, axis=-1, keepdims=True) + jnp.sum(p_ctx, axis=-1, keepdims=True)
                 + jnp.exp2(sink - m))
            acc = (_dot(p_loc.astype(jnp.bfloat16), vband[:, sel])
                   + _dot(p_ctx.astype(jnp.bfloat16), vctx[:, sel]))
            outs.append(acc / l)
        o_ref[0, :, LANES * pr:LANES * (pr + 1)] = jnp.where(low_half, outs[0], outs[1]).astype(o_ref.dtype)


def _attn_a_call(sink, qa, ka, va, n_lat, nqb):
    bsz, t, _ = qa.shape
    return pl.pallas_call(
        functools.partial(_attn_a_kernel, n_lat=n_lat),
        grid=(bsz, nqb),
        in_specs=[pl.BlockSpec(memory_space=pltpu.SMEM),
                  pl.BlockSpec((1, TQ, 384), lambda b, i: (b, i, 0)),
                  pl.BlockSpec((1, t, 256), lambda b, i: (b, 0, 0)),
                  pl.BlockSpec((1, t, 256), lambda b, i: (b, 0, 0))],
        out_specs=pl.BlockSpec((1, TQ, 384), lambda b, i: (b, i, 0)),
        out_shape=jax.ShapeDtypeStruct((bsz, nqb * TQ, 384), jnp.bfloat16),
        compiler_params=pltpu.CompilerParams(dimension_semantics=("arbitrary", "arbitrary"),
                                             vmem_limit_bytes=VMEM_LIMIT),
        name="attn_a",
    )(sink, qa, ka, va)


def _flash_init(m_ref, l_ref, acc_ref):
    m_ref[...] = jnp.full(m_ref.shape, NEG_INF, jnp.float32)
    l_ref[...] = jnp.zeros(l_ref.shape, jnp.float32)
    acc_ref[...] = jnp.zeros(acc_ref.shape, jnp.float32)


def _flash_scores(qm, kts):
    parts = []
    for kt in kts:
        s = _dot(qm, kt)
        parts += [s[:, k * LANES:(k + 1) * LANES] for k in range(TK // LANES)]
    return parts


def _flash_update(j, m_ref, l_ref, acc_ref, parts, v):
    m_old = m_ref[j]
    m_new = jnp.maximum(m_old, jnp.max(functools.reduce(jnp.maximum, parts), axis=1, keepdims=True))
    alpha = jnp.exp2(m_old - m_new)
    ps = [jnp.exp2(part - m_new) for part in parts]
    l_ref[j] = alpha * l_ref[j] + functools.reduce(jnp.add, ps)
    pb = jnp.concatenate([p.astype(jnp.bfloat16) for p in ps], axis=1)
    acc_ref[j] = alpha * acc_ref[j] + _dot(pb, v)
    m_ref[j] = m_new


def _key_steps(n_tiles):
    return [(c, min(SUB, n_tiles - c)) for c in range(0, n_tiles, SUB)]


def _flash_run(m_ref, l_ref, acc_ref, qms, steps, kts_of, v_of):
    seq = [(c0, n, j) for (c0, n) in steps for j in range(len(qms))]
    scores = lambda item: _flash_scores(qms[item[2]], kts_of(*item))
    pending = [scores(item) for item in seq[:LOOKAHEAD]]
    for idx, (c0, n, j) in enumerate(seq):
        if idx + LOOKAHEAD < len(seq):
            pending.append(scores(seq[idx + LOOKAHEAD]))
        _flash_update(j, m_ref, l_ref, acc_ref, pending.pop(0), v_of(c0, n, j))


def _flash_result(j, l_ref, acc_ref):
    return acc_ref[j] / jnp.sum(l_ref[j], axis=1, keepdims=True)


def _flash_scratch(n_maps):
    return [pltpu.VMEM((n_maps, TQ, LANES), jnp.float32) for _ in range(3)]


def _attn_b_kernel(lam_ref, sub_ref, q_ref, kt_ref, v_ref, o_ref, m_ref, l_ref, acc_ref, *, n_lat, lam_init):
    i = pl.program_id(1)
    n_lat_tiles = n_lat // TK
    is_ctx = i == n_lat // TQ
    lam4 = lam_ref[...]
    lam = (jnp.exp(jnp.sum(lam4[0:1] * lam4[1:2], axis=-1, keepdims=True))
           - jnp.exp(jnp.sum(lam4[2:3] * lam4[3:4], axis=-1, keepdims=True)) + lam_init)

    lane = lax.broadcasted_iota(jnp.int32, (TQ, LANES), 1)
    n_maps = 2 * B_HEADS
    qms = []
    for j in range(n_maps):
        qg = q_ref[0, :, LANES * (j // 4):LANES * (j // 4 + 1)]
        qms.append(jnp.where((lane // B_QK_DIM) == (j % 4), qg, jnp.zeros_like(qg)))

    _flash_init(m_ref, l_ref, acc_ref)

    def kts_of(c0, n, j):
        g = j // 4
        return [kt_ref[0, c0 + k, LANES * g:LANES * (g + 1), :] for k in range(n)]

    def v_of(c0, n, j):
        g = j // 4
        return v_ref[0, c0 * TK:(c0 + n) * TK, LANES * g:LANES * (g + 1)]

    @pl.when(is_ctx)
    def _():
        _flash_run(m_ref, l_ref, acc_ref, qms, [(n_lat_tiles, 1)], kts_of, v_of)

    @pl.when(jnp.logical_not(is_ctx))
    def _():
        _flash_run(m_ref, l_ref, acc_ref, qms, _key_steps(n_lat_tiles + 1), kts_of, v_of)

    low_half = lane < B_V_DIM
    for g in range(2):
        heads = []
        for e in range(2):
            heads.append(_flash_result(4 * g + 2 * e, l_ref, acc_ref)
                         - lam * _flash_result(4 * g + 2 * e + 1, l_ref, acc_ref))
        o = jnp.where(low_half, heads[0], heads[1])
        sq = o * o
        ms_lo = jnp.sum(jnp.where(low_half, sq, 0.0), axis=-1, keepdims=True) * (1.0 / B_V_DIM)
        ms_hi = jnp.sum(jnp.where(low_half, 0.0, sq), axis=-1, keepdims=True) * (1.0 / B_V_DIM)
        ms = jnp.where(low_half, ms_lo, ms_hi)
        y = o * lax.rsqrt(ms + NORM_EPS) * sub_ref[...] * (1.0 - lam_init)
        o_ref[0, :, LANES * g:LANES * (g + 1)] = y.astype(o_ref.dtype)


def _attn_b_call(lam4, sub, qb, kbt, vb, n_lat, lam_init, nqb):
    bsz, t, _ = qb.shape
    nb = t // TQ
    return pl.pallas_call(
        functools.partial(_attn_b_kernel, n_lat=n_lat, lam_init=lam_init),
        grid=(bsz, nqb),
        in_specs=[pl.BlockSpec((4, B_QK_DIM), lambda b, i: (0, 0)),
                  pl.BlockSpec((1, LANES), lambda b, i: (0, 0)),
                  pl.BlockSpec((1, TQ, 256), lambda b, i: (b, i, 0)),
                  pl.BlockSpec((1, nb, 256, TK), lambda b, i: (b, 0, 0, 0)),
                  pl.BlockSpec((1, t, 256), lambda b, i: (b, 0, 0))],
        out_specs=pl.BlockSpec((1, TQ, 256), lambda b, i: (b, i, 0)),
        out_shape=jax.ShapeDtypeStruct((bsz, nqb * TQ, 256), jnp.bfloat16),
        scratch_shapes=_flash_scratch(2 * B_HEADS),
        compiler_params=pltpu.CompilerParams(dimension_semantics=("arbitrary", "arbitrary"),
                                             vmem_limit_bytes=VMEM_LIMIT),
        name="attn_b",
    )(lam4, sub, qb, kbt, vb)


def _attn_c_kernel(q_ref, kt_ref, v_ref, o_ref, m_ref, l_ref, acc_ref, *, n_lat):
    i = pl.program_id(1)
    n_lat_tiles = n_lat // TK
    is_ctx = i == n_lat // TQ
    lane2 = lax.broadcasted_iota(jnp.int32, (TQ, 2 * LANES), 1)
    even = (lane2 < 64) | ((lane2 >= 128) & (lane2 < 160))
    odd = ((lane2 >= 64) & (lane2 < 128)) | ((lane2 >= 160) & (lane2 < 192))
    qms = []
    for h in range(C_HEADS):
        qp = q_ref[0, :, 256 * (h // 2):256 * (h // 2 + 1)]
        qms.append(jnp.where(even if h % 2 == 0 else odd, qp, jnp.zeros_like(qp)))

    _flash_init(m_ref, l_ref, acc_ref)

    def kts_of(c0, n, h):
        pr = h // 2
        return [kt_ref[0, c0 + k, 256 * pr:256 * (pr + 1), :] for k in range(n)]

    def v_of(c0, n, h):
        pr = h // 2
        return v_ref[0, c0 * TK:(c0 + n) * TK, LANES * pr:LANES * (pr + 1)]

    @pl.when(is_ctx)
    def _():
        _flash_run(m_ref, l_ref, acc_ref, qms, [(n_lat_tiles, 1)], kts_of, v_of)

    @pl.when(jnp.logical_not(is_ctx))
    def _():
        _flash_run(m_ref, l_ref, acc_ref, qms, _key_steps(n_lat_tiles + 1), kts_of, v_of)

    low_half = lax.broadcasted_iota(jnp.int32, (TQ, LANES), 1) < C_V
    for pr in range(C_HEADS // 2):
        o = jnp.where(low_half, _flash_result(2 * pr, l_ref, acc_ref), _flash_result(2 * pr + 1, l_ref, acc_ref))
        o_ref[0, :, LANES * pr:LANES * (pr + 1)] = o.astype(o_ref.dtype)


def _attn_c_call(qc, kct, vc, n_lat, nqb):
    bsz, t, _ = qc.shape
    nb = t // TQ
    return pl.pallas_call(
        functools.partial(_attn_c_kernel, n_lat=n_lat),
        grid=(bsz, nqb),
        in_specs=[pl.BlockSpec((1, TQ, 768), lambda b, i: (b, i, 0)),
                  pl.BlockSpec((1, nb, 768, TK), lambda b, i: (b, 0, 0, 0)),
                  pl.BlockSpec((1, t, 384), lambda b, i: (b, 0, 0))],
        out_specs=pl.BlockSpec((1, TQ, 384), lambda b, i: (b, i, 0)),
        out_shape=jax.ShapeDtypeStruct((bsz, nqb * TQ, 384), jnp.bfloat16),
        scratch_shapes=_flash_scratch(C_HEADS),
        compiler_params=pltpu.CompilerParams(dimension_semantics=("arbitrary", "arbitrary"),
                                             vmem_limit_bytes=VMEM_LIMIT),
        name="attn_c",
    )(qc, kct, vc)


def _post_kernel(x_ref, mod_ref, oa_ref, ob_ref, oc_ref, wout_ref, gpm_ref, gpf_ref, gqf_ref,
                 wg_ref, wu_ref, wd_ref, o_ref):
    mod = mod_ref[0]
    x = x_ref[0]
    mix = jnp.concatenate([oa_ref[0], ob_ref[0], oc_ref[0]], axis=1)
    x = x + mod[2:3] * _rms(_dot(mix, wout_ref[...]), gpm_ref[...])
    h = (_rms(x, gpf_ref[...]) * (1.0 + mod[4:5]) + mod[3:4]).astype(jnp.bfloat16)
    g = _dot(h, wg_ref[...])
    u = _dot(h, wu_ref[...])
    act = (g * jax.nn.sigmoid(g) * u).astype(jnp.bfloat16)
    ff = _dot(act, wd_ref[...])
    o_ref[0] = x + mod[5:6] * _rms(ff, gqf_ref[...])


def _post_call(x, mods, oa, ob, oc, wout, gpm, gpf, gqf, wg, wu, wd, *, tm, blk0, mod_row):
    bsz, n, d = x.shape
    tok = lambda w: pl.BlockSpec((1, tm, w), lambda i, b: (b, i, 0))
    att = lambda w: pl.BlockSpec((1, tm, w), lambda i, b: (b, blk0 + i, 0))
    mod_spec = pl.BlockSpec((1, 6, d), lambda i, b: (2 * b + mod_row, 0, 0))
    return pl.pallas_call(
        _post_kernel,
        grid=(n // tm, bsz),
        in_specs=[tok(d), mod_spec, att(384), att(256), att(384),
                  _const_spec(wout.shape), _const_spec((1, d)), _const_spec((1, d)), _const_spec((1, d)),
                  _const_spec(wg.shape), _const_spec(wu.shape), _const_spec(wd.shape)],
        out_specs=tok(d),
        out_shape=jax.ShapeDtypeStruct((bsz, n, d), jnp.float32),
        compiler_params=pltpu.CompilerParams(dimension_semantics=("arbitrary", "arbitrary"),
                                             vmem_limit_bytes=VMEM_LIMIT),
        name="post",
    )(x, mods, oa, ob, oc, wout, gpm, gpf, gqf, wg, wu, wd)


def kernel(x, c, ctx, c_ctx, w_ada, b_ada, g_pre_mix, g_post_mix, w_in, win_sink, diff_lambda_q1, diff_lambda_k1, diff_lambda_q2, diff_lambda_k2, diff_sub_norm, mla_q_norm, mla_w_q_up, mla_kv_norm, mla_w_kv_up, w_out, g_pre_ffn, g_post_ffn, w_gate, w_up, w_down):
    bsz, n_lat, d = x.shape
    n_ctx = ctx.shape[1]
    depth = w_ada.shape[0]
    assert d == D_MODEL and n_lat % TM_POST == 0 and n_ctx == TQ and n_lat >= BAND
    n_lat_blocks = n_lat // TQ
    bf = jnp.bfloat16

    cvec = jnp.zeros((8, d), jnp.float32).at[:bsz].set(c).at[bsz].set(c_ctx)
    mod_all = _ada_call(cvec, w_ada, b_ada)
    tables = _rope_tables(n_lat, n_ctx)

    ext_cols = _ext_columns()
    qup_cols = _qup_columns()
    kvup_cols = _kvup_columns()

    for l in range(depth):
        update_ctx = l < depth - 1
        m = mod_all[l].reshape(8, 6, d)
        mods = jnp.stack([m[:bsz], jnp.broadcast_to(m[bsz], (bsz, 6, d))], axis=1).reshape(2 * bsz, 6, d)

        w_ext = jnp.concatenate([w_in[l], jnp.zeros((d, 1), jnp.float32)], axis=1)[:, ext_cols].astype(bf)
        wq = jnp.concatenate([mla_w_q_up[l], jnp.zeros((C_Q_RANK, 1), jnp.float32)], axis=1)[:, qup_cols].astype(bf)
        wkv = mla_w_kv_up[l][:, kvup_cols].astype(bf)

        qa, ka, va, qb, kbt, vb, qc, kct, vc = _proj_call(
            x, ctx, mods, g_pre_mix[l][None], w_ext, mla_q_norm[l][None], wq, mla_kv_norm[l][None], wkv, tables)

        lam_init = 0.8 - 0.6 * math.exp(-0.3 * l)
        lam4 = jnp.stack([diff_lambda_q1[l], diff_lambda_k1[l], diff_lambda_q2[l], diff_lambda_k2[l]])
        sub = jnp.concatenate([diff_sub_norm[l], diff_sub_norm[l]])[None]

        nqb = n_lat_blocks + (1 if update_ctx else 0)
        oa = _attn_a_call(win_sink[l], qa, ka, va, n_lat, nqb)
        ob = _attn_b_call(lam4, sub, qb, kbt, vb, n_lat, lam_init, nqb)
        oc = _attn_c_call(qc, kct, vc, n_lat, nqb)

        weights = (w_out[l].astype(bf), g_post_mix[l][None], g_pre_ffn[l][None], g_post_ffn[l][None],
                   w_gate[l].astype(bf), w_up[l].astype(bf), w_down[l].astype(bf))
        x_new = _post_call(x, mods, oa, ob, oc, *weights, tm=TM_POST, blk0=0, mod_row=0)
        if update_ctx:
            ctx = _post_call(ctx, mods, oa, ob, oc, *weights, tm=TQ, blk0=n_lat_blocks, mod_row=1)
        x = x_new
    return x
```

```python
import functools
import math

import numpy as np
import jax
import jax.numpy as jnp
from jax import lax
from jax.experimental import pallas as pl
from jax.experimental.pallas import tpu as pltpu

D_MODEL = 1024
GRID_W = 64
HEAD_DIM = 64
A_HEADS = 6
A_KV_HEADS = 2
A_GROUP = A_HEADS // A_KV_HEADS
WINDOW = 128
B_HEADS = 4
B_QK_DIM = 32
B_V_DIM = 64
C_HEADS = 6
C_Q_RANK = 256
C_KV_RANK = 128
C_NOPE = 64
C_ROPE = 32
C_V = 64
D_FF = 2816
ROPE_BASE = 10000.0
NORM_EPS = 1e-6
NEG_INF = -1e30
LOG2E = math.log2(math.e)
A_SCALE = HEAD_DIM ** -0.5 * LOG2E
B_SCALE = B_QK_DIM ** -0.5 * LOG2E
C_SCALE = (C_NOPE + C_ROPE) ** -0.5 * LOG2E

LANES = 128
TQ = 256
TM_POST = 512
TK = 256
SUB = 11
LOOKAHEAD = 1
GROUP = 11
ROW_SUM_LIMIT = 2.0 ** 100
BAND = TQ + 2 * WINDOW
VMEM_LIMIT = 56 * 1024 * 1024

_AQ, _AK, _AV = 0, 384, 512
_DQ, _DK, _DV = 640, 896, 1152
_MQ, _MKV, _MKR = 1408, 1664, 1792
_ZERO_COL = 1824

N_ROPED = 384 + 256 + 256 + 256 + 128
N_PLAIN = 256 + 256 + 256 + 128
N_EXT = 2 * N_ROPED + N_PLAIN


def _rope_partner(unit):
    half = unit // 2
    quarter = half // 2
    d = np.arange(unit)
    r = d % half
    partner = np.where(r < quarter, d + quarter, d - quarter)
    sign = np.where(r < quarter, -1.0, 1.0).astype(np.float32)
    return partner, sign


def _ext_columns():
    pa, _ = _rope_partner(HEAD_DIM)
    pb, _ = _rope_partner(B_QK_DIM)

    def rot(src, unit, partner):
        return (src // unit) * unit + partner[src % unit]

    qa = np.arange(A_HEADS * HEAD_DIM)
    dup = np.concatenate([np.arange(64), np.arange(64), 64 + np.arange(64), 64 + np.arange(64)])
    qb = np.arange(256)
    kr = np.concatenate([np.arange(32), np.arange(32)])
    zeros64 = np.full((64,), _ZERO_COL)
    main = [_AQ + qa, _AK + dup, _DQ + qb, _DK + qb, np.concatenate([_MKR + kr, zeros64])]
    rots = [_AQ + rot(qa, 64, pa), _AK + rot(dup, 64, pa), _DQ + rot(qb, 32, pb), _DK + rot(qb, 32, pb),
            np.concatenate([_MKR + rot(kr, 32, pb), zeros64])]
    plain = [_AV + dup, _DV + qb, _MQ + np.arange(256), _MKV + np.arange(128)]
    cols = np.concatenate(main + rots + plain)
    assert cols.shape[0] == N_EXT
    return cols


def _qup_columns():
    pb, _ = _rope_partner(C_ROPE)
    width = C_NOPE + C_ROPE
    zero = C_HEADS * width
    zeros64 = np.full((64,), zero)
    main, rots = [], []
    for p in range(C_HEADS // 2):
        h0, h1 = 2 * p, 2 * p + 1
        main += [h0 * width + np.arange(64), h1 * width + np.arange(64),
                 h0 * width + 64 + np.arange(32), h1 * width + 64 + np.arange(32), zeros64]
        rots += [h0 * width + 64 + pb, h1 * width + 64 + pb, zeros64]
    return np.concatenate(main + rots)


def _kvup_columns():
    width = C_NOPE + C_V
    k = np.concatenate([h * width + np.arange(64) for h in range(C_HEADS)])
    v = np.concatenate([h * width + 64 + np.arange(64) for h in range(C_HEADS)])
    return np.concatenate([k, v])


def _rope_tables(n_lat, n_ctx):
    t = jnp.arange(n_lat)
    row = (t // GRID_W).astype(jnp.float32)
    col = (t % GRID_W).astype(jnp.float32)

    def table(unit):
        half = unit // 2
        quarter = half // 2
        _, sign = _rope_partner(unit)
        lane = np.arange(LANES)
        d = lane % unit
        axis = d // half
        j = d % quarter
        freqs = ROPE_BASE ** (-jnp.arange(0, half, 2, dtype=jnp.float32) / half)
        f = freqs[j]
        pos = jnp.where(jnp.asarray(axis)[None, :] == 0, row[:, None], col[:, None])
        ang = pos * f[None, :]
        cos = jnp.cos(ang)
        sin = jnp.sin(ang) * jnp.asarray(sign[d])[None, :]
        cos = jnp.concatenate([cos, jnp.ones((n_ctx, LANES), jnp.float32)], axis=0)
        sin = jnp.concatenate([sin, jnp.zeros((n_ctx, LANES), jnp.float32)], axis=0)
        return cos, sin

    cos_a, sin_a = table(HEAD_DIM)
    cos_b, sin_b = table(B_QK_DIM)
    return cos_a, sin_a, cos_b, sin_b


def _rms(x, g):
    return x * lax.rsqrt(jnp.mean(x * x, axis=-1, keepdims=True) + NORM_EPS) * g


def _dot(a, b):
    return jnp.dot(a, b, preferred_element_type=jnp.float32)


def _dot_nt(a, b):
    return lax.dot_general(a, b, (((1,), (1,)), ((), ())), preferred_element_type=jnp.float32)


def _ada_kernel(c_ref, w_ref, b_ref, o_ref):
    c = c_ref[...]
    h = c * jax.nn.sigmoid(c)
    o_ref[0] = _dot(h, w_ref[0]) + b_ref[0]


def _ada_call(cvec, w_ada, b_ada):
    depth, d, n = w_ada.shape
    tn = 1536
    return pl.pallas_call(
        _ada_kernel,
        grid=(depth, n // tn),
        in_specs=[pl.BlockSpec((8, d), lambda l, j: (0, 0)),
                  pl.BlockSpec((1, d, tn), lambda l, j: (l, 0, j)),
                  pl.BlockSpec((1, 1, tn), lambda l, j: (l, 0, j))],
        out_specs=pl.BlockSpec((1, 8, tn), lambda l, j: (l, 0, j)),
        out_shape=jax.ShapeDtypeStruct((depth, 8, n), jnp.float32),
        compiler_params=pltpu.CompilerParams(dimension_semantics=("arbitrary", "arbitrary"),
                                             vmem_limit_bytes=VMEM_LIMIT),
        name="adaln",
    )(cvec, w_ada, b_ada.reshape(depth, 1, n))


def _proj_kernel(x_ref, ctx_ref, mod_ref, gpre_ref, w_ref, qn_ref, wq_ref, kvn_ref, wkv_ref,
                 ca_ref, sa_ref, cb_ref, sb_ref,
                 qa_ref, ka_ref, va_ref, qb_ref, kbt_ref, vb_ref, qc_ref, kct_ref, vc_ref, *, n_lat_blocks):
    mod = mod_ref[0]
    x = jnp.where(pl.program_id(1) == n_lat_blocks, ctx_ref[0], x_ref[0])
    h = _rms(x, gpre_ref[...]) * (1.0 + mod[1:2]) + mod[0:1]
    p = _dot(h.astype(jnp.bfloat16), w_ref[...])

    ca, sa, cb, sb = ca_ref[...], sa_ref[...], cb_ref[...], sb_ref[...]

    def roped(off, width, cos, sin):
        n = width // LANES
        main = p[:, off:off + width]
        rot = p[:, N_ROPED + off:N_ROPED + off + width]
        return main * jnp.concatenate([cos] * n, axis=1) + rot * jnp.concatenate([sin] * n, axis=1)

    qa_ref[0] = (roped(0, 384, ca, sa) * A_SCALE).astype(jnp.bfloat16)
    ka_ref[0] = roped(384, 256, ca, sa).astype(jnp.bfloat16)
    qb_ref[0] = (roped(640, 256, cb, sb) * B_SCALE).astype(jnp.bfloat16)
    kbt_ref[0, 0] = roped(896, 256, cb, sb).T.astype(jnp.bfloat16)
    kr = roped(1152, 128, cb, sb)

    o = 2 * N_ROPED
    va_ref[0] = p[:, o:o + 256].astype(jnp.bfloat16)
    vb_ref[0] = p[:, o + 256:o + 512].astype(jnp.bfloat16)
    cq = p[:, o + 512:o + 768]
    ckv = p[:, o + 768:o + 896]

    qc = _dot(_rms(cq, qn_ref[...]).astype(jnp.bfloat16), wq_ref[...])
    kv = _dot(_rms(ckv, kvn_ref[...]).astype(jnp.bfloat16), wkv_ref[...])
    vc_ref[0] = kv[:, 384:].astype(jnp.bfloat16)
    kr_t = kr.T.astype(jnp.bfloat16)
    for pr in range(C_HEADS // 2):
        nope = qc[:, 256 * pr:256 * pr + 128]
        rope = (qc[:, 256 * pr + 128:256 * pr + 256] * cb
                + qc[:, 768 + 128 * pr:768 + 128 * pr + 128] * sb)
        qc_ref[0, :, 256 * pr:256 * pr + 128] = (nope * C_SCALE).astype(jnp.bfloat16)
        qc_ref[0, :, 256 * pr + 128:256 * pr + 256] = (rope * C_SCALE).astype(jnp.bfloat16)
        kct_ref[0, 0, 256 * pr:256 * pr + 128, :] = kv[:, 128 * pr:128 * pr + 128].T.astype(jnp.bfloat16)
        kct_ref[0, 0, 256 * pr + 128:256 * pr + 256, :] = kr_t


def _const_spec(shape):
    nd = len(shape)
    return pl.BlockSpec(shape, lambda *_: (0,) * nd, pipeline_mode=pl.Buffered(1))


def _proj_call(x, ctx, mods, gpre, w_ext, qn, wq, kvn, wkv, tables):
    bsz, n_lat, d = x.shape
    n_lat_blocks = n_lat // TQ
    nb = n_lat_blocks + 1
    t = nb * TQ
    tok = lambda w: pl.BlockSpec((1, TQ, w), lambda b, i: (b, i, 0))
    tbl = pl.BlockSpec((TQ, LANES), lambda b, i: (i, 0))
    kt = lambda r: pl.BlockSpec((1, 1, r, TQ), lambda b, i: (b, i, 0, 0))
    x_spec = pl.BlockSpec((1, TQ, d), lambda b, i: (b, jnp.minimum(i, n_lat_blocks - 1), 0))
    ctx_spec = pl.BlockSpec((1, TQ, d), lambda b, i: (b, 0, 0))
    mod_spec = pl.BlockSpec((1, 6, d), lambda b, i: (2 * b + (i >= n_lat_blocks).astype(jnp.int32), 0, 0))
    bf = jnp.bfloat16
    out_shape = [jax.ShapeDtypeStruct((bsz, t, 384), bf), jax.ShapeDtypeStruct((bsz, t, 256), bf),
                 jax.ShapeDtypeStruct((bsz, t, 256), bf), jax.ShapeDtypeStruct((bsz, t, 256), bf),
                 jax.ShapeDtypeStruct((bsz, nb, 256, TQ), bf), jax.ShapeDtypeStruct((bsz, t, 256), bf),
                 jax.ShapeDtypeStruct((bsz, t, 768), bf), jax.ShapeDtypeStruct((bsz, nb, 768, TQ), bf),
                 jax.ShapeDtypeStruct((bsz, t, 384), bf)]
    out_specs = [tok(384), tok(256), tok(256), tok(256), kt(256), tok(256), tok(768), kt(768), tok(384)]
    return pl.pallas_call(
        functools.partial(_proj_kernel, n_lat_blocks=n_lat_blocks),
        grid=(bsz, nb),
        in_specs=[x_spec, ctx_spec, mod_spec, _const_spec((1, d)), _const_spec(w_ext.shape),
                  _const_spec(qn.shape), _const_spec(wq.shape), _const_spec(kvn.shape), _const_spec(wkv.shape),
                  tbl, tbl, tbl, tbl],
        out_specs=out_specs,
        out_shape=out_shape,
        compiler_params=pltpu.CompilerParams(dimension_semantics=("arbitrary", "arbitrary"),
                                             vmem_limit_bytes=VMEM_LIMIT),
        name="proj",
    )(x, ctx, mods, gpre, w_ext, qn, wq, kvn, wkv, *tables)


def _attn_a_kernel(sink_ref, q_ref, k_ref, v_ref, o_ref, *, n_lat):
    i = pl.program_id(1)
    is_ctx = i == n_lat // TQ
    t0 = i * TQ
    start = pl.multiple_of(jnp.clip(t0 - WINDOW, 0, n_lat - BAND), WINDOW)
    kband = k_ref[0, pl.ds(start, BAND), :]
    vband = v_ref[0, pl.ds(start, BAND), :]
    kctx = k_ref[0, n_lat:, :]
    vctx = v_ref[0, n_lat:, :]
    qpos = t0 + jnp.where(is_ctx, 4 * n_lat, 0) + lax.broadcasted_iota(jnp.int32, (TQ, BAND), 0)
    kpos = start + lax.broadcasted_iota(jnp.int32, (TQ, BAND), 1)
    mask = jnp.abs(qpos - kpos) <= WINDOW
    low_half = lax.broadcasted_iota(jnp.int32, (TQ, LANES), 1) < HEAD_DIM

    for pr in range(A_HEADS // 2):
        qg = q_ref[0, :, LANES * pr:LANES * (pr + 1)]
        outs = []
        for e in range(2):
            head = 2 * pr + e
            kv = head // A_GROUP
            sel = slice(LANES * kv, LANES * (kv + 1))
            qm = jnp.where(low_half if e == 0 else jnp.logical_not(low_half), qg, jnp.zeros_like(qg))
            s_loc = jnp.where(mask, _dot_nt(qm, kband[:, sel]), NEG_INF)
            s_ctx = _dot_nt(qm, kctx[:, sel])
            sink = sink_ref[head] * LOG2E
            m = jnp.maximum(jnp.maximum(jnp.max(s_loc, axis=-1, keepdims=True),
                                        jnp.max(s_ctx, axis=-1, keepdims=True)), sink)
            p_loc = jnp.exp2(s_loc - m)
            p_ctx = jnp.exp2(s_ctx - m)
            l = (jnp.sum(p_loc, axis=-1, keepdims=True) + jnp.sum(p_ctx, axis=-1, keepdims=True)
                 + jnp.exp2(sink - m))
            acc = (_dot(p_loc.astype(jnp.bfloat16), vband[:, sel])
                   + _dot(p_ctx.astype(jnp.bfloat16), vctx[:, sel]))
            outs.append(acc / l)
        o_ref[0, :, LANES * pr:LANES * (pr + 1)] = jnp.where(low_half, outs[0], outs[1]).astype(o_ref.dtype)


def _attn_a_call(sink, qa, ka, va, n_lat, nqb):
    bsz, t, _ = qa.shape
    return pl.pallas_call(
        functools.partial(_attn_a_kernel, n_lat=n_lat),
        grid=(bsz, nqb),
        in_specs=[pl.BlockSpec(memory_space=pltpu.SMEM),
                  pl.BlockSpec((1, TQ, 384), lambda b, i: (b, i, 0)),
                  pl.BlockSpec((1, t, 256), lambda b, i: (b, 0, 0)),
                  pl.BlockSpec((1, t, 256), lambda b, i: (b, 0, 0))],
        out_specs=pl.BlockSpec((1, TQ, 384), lambda b, i: (b, i, 0)),
        out_shape=jax.ShapeDtypeStruct((bsz, nqb * TQ, 384), jnp.bfloat16),
        compiler_params=pltpu.CompilerParams(dimension_semantics=("arbitrary", "arbitrary"),
                                             vmem_limit_bytes=VMEM_LIMIT),
        name="attn_a",
    )(sink, qa, ka, va)


def _flash_init(m_ref, l_ref, acc_ref):
    m_ref[...] = jnp.full(m_ref.shape, NEG_INF, jnp.float32)
    l_ref[...] = jnp.zeros(l_ref.shape, jnp.float32)
    acc_ref[...] = jnp.zeros(acc_ref.shape, jnp.float32)


def _flash_scores(qm, kts):
    parts = []
    for kt in kts:
        s = _dot(qm, kt)
        parts += [s[:, k * LANES:(k + 1) * LANES] for k in range(TK // LANES)]
    return parts


def _flash_update(j, m_ref, l_ref, acc_ref, parts, v):
    m_old = m_ref[j]
    m_new = jnp.maximum(m_old, jnp.max(functools.reduce(jnp.maximum, parts), axis=1, keepdims=True))
    alpha = jnp.exp2(m_old - m_new)
    ps = [jnp.exp2(part - m_new) for part in parts]
    l_ref[j] = alpha * l_ref[j] + functools.reduce(jnp.add, ps)
    pb = jnp.concatenate([p.astype(jnp.bfloat16) for p in ps], axis=1)
    acc_ref[j] = alpha * acc_ref[j] + _dot(pb, v)
    m_ref[j] = m_new


def _key_steps(n_tiles, size):
    return [(c, min(size, n_tiles - c)) for c in range(0, n_tiles, size)]


def _flash_run(m_ref, l_ref, acc_ref, qms, steps, kts_of, v_of):
    seq = [(c0, n, j) for (c0, n) in steps for j in range(len(qms))]
    scores = lambda item: _flash_scores(qms[item[2]], kts_of(*item))
    pending = [scores(item) for item in seq[:LOOKAHEAD]]
    for idx, (c0, n, j) in enumerate(seq):
        if idx + LOOKAHEAD < len(seq):
            pending.append(scores(seq[idx + LOOKAHEAD]))
        _flash_update(j, m_ref, l_ref, acc_ref, pending.pop(0), v_of(c0, n, j))


def _single_pass_run(acc_ref, qms, groups, kts_of, v_of):
    seq = [(j, gi) for j in range(len(qms)) for gi in range(len(groups))]
    scores = lambda item: _flash_scores(qms[item[0]], kts_of(*groups[item[1]], item[0]))
    pending = [scores(seq[0])]
    l_max = None
    for idx, (j, gi) in enumerate(seq):
        if idx + 1 < len(seq):
            pending.append(scores(seq[idx + 1]))
        parts = pending.pop(0)
        if gi == 0:
            shift = jnp.max(functools.reduce(jnp.maximum, parts[:TK // LANES]), axis=1, keepdims=True)
            l = acc = None
        ps = [jnp.exp2(part - shift) for part in parts]
        l_g = functools.reduce(jnp.add, ps)
        pb = jnp.concatenate([p.astype(jnp.bfloat16) for p in ps], axis=1)
        pv = _dot(pb, v_of(*groups[gi], j))
        l, acc = (l_g, pv) if gi == 0 else (l + l_g, acc + pv)
        if gi == len(groups) - 1:
            l_row = jnp.sum(l, axis=1, keepdims=True)
            acc_ref[j] = acc / l_row
            l_max = jnp.max(l_row) if l_max is None else jnp.maximum(l_max, jnp.max(l_row))
    return l_max


def _attend(m_ref, l_ref, acc_ref, qms, is_ctx, n_lat_tiles, kts_of, v_of):
    @pl.when(is_ctx)
    def _():
        _single_pass_run(acc_ref, qms, [(n_lat_tiles, 1)], kts_of, v_of)

    @pl.when(jnp.logical_not(is_ctx))
    def _():
        l_max = _single_pass_run(acc_ref, qms, _key_steps(n_lat_tiles + 1, GROUP), kts_of, v_of)

        @pl.when(jnp.logical_not(l_max < ROW_SUM_LIMIT))
        def _():
            _flash_init(m_ref, l_ref, acc_ref)
            _flash_run(m_ref, l_ref, acc_ref, qms, _key_steps(n_lat_tiles + 1, SUB), kts_of, v_of)
            for j in range(len(qms)):
                acc_ref[j] = acc_ref[j] / jnp.sum(l_ref[j], axis=1, keepdims=True)


def _resident_spec(shape):
    nd = len(shape)
    return pl.BlockSpec(shape, lambda b, i: (b,) + (0,) * (nd - 1), pipeline_mode=pl.Buffered(1))


def _flash_scratch(n_maps):
    return [pltpu.VMEM((n_maps, TQ, LANES), jnp.float32) for _ in range(3)]


def _attn_b_kernel(lam_ref, sub_ref, q_ref, kt_ref, v_ref, o_ref, m_ref, l_ref, acc_ref, *, n_lat, lam_init):
    i = pl.program_id(1)
    n_lat_tiles = n_lat // TK
    is_ctx = i == n_lat // TQ
    lam4 = lam_ref[...]
    lam = (jnp.exp(jnp.sum(lam4[0:1] * lam4[1:2], axis=-1, keepdims=True))
           - jnp.exp(jnp.sum(lam4[2:3] * lam4[3:4], axis=-1, keepdims=True)) + lam_init)

    lane = lax.broadcasted_iota(jnp.int32, (TQ, LANES), 1)
    n_maps = 2 * B_HEADS
    qms = []
    for j in range(n_maps):
        qg = q_ref[0, :, LANES * (j // 4):LANES * (j // 4 + 1)]
        qms.append(jnp.where((lane // B_QK_DIM) == (j % 4), qg, jnp.zeros_like(qg)))

    def kts_of(c0, n, j):
        g = j // 4
        return [kt_ref[0, c0 + k, LANES * g:LANES * (g + 1), :] for k in range(n)]

    def v_of(c0, n, j):
        g = j // 4
        return v_ref[0, c0 * TK:(c0 + n) * TK, LANES * g:LANES * (g + 1)]

    _attend(m_ref, l_ref, acc_ref, qms, is_ctx, n_lat_tiles, kts_of, v_of)

    low_half = lane < B_V_DIM
    for g in range(2):
        heads = []
        for e in range(2):
            heads.append(acc_ref[4 * g + 2 * e] - lam * acc_ref[4 * g + 2 * e + 1])
        o = jnp.where(low_half, heads[0], heads[1])
        sq = o * o
        ms_lo = jnp.sum(jnp.where(low_half, sq, 0.0), axis=-1, keepdims=True) * (1.0 / B_V_DIM)
        ms_hi = jnp.sum(jnp.where(low_half, 0.0, sq), axis=-1, keepdims=True) * (1.0 / B_V_DIM)
        ms = jnp.where(low_half, ms_lo, ms_hi)
        y = o * lax.rsqrt(ms + NORM_EPS) * sub_ref[...] * (1.0 - lam_init)
        o_ref[0, :, LANES * g:LANES * (g + 1)] = y.astype(o_ref.dtype)


def _attn_b_call(lam4, sub, qb, kbt, vb, n_lat, lam_init, nqb):
    bsz, t, _ = qb.shape
    nb = t // TQ
    return pl.pallas_call(
        functools.partial(_attn_b_kernel, n_lat=n_lat, lam_init=lam_init),
        grid=(bsz, nqb),
        in_specs=[pl.BlockSpec((4, B_QK_DIM), lambda b, i: (0, 0)),
                  pl.BlockSpec((1, LANES), lambda b, i: (0, 0)),
                  pl.BlockSpec((1, TQ, 256), lambda b, i: (b, i, 0)),
                  _resident_spec((1, nb, 256, TK)), _resident_spec((1, t, 256))],
        out_specs=pl.BlockSpec((1, TQ, 256), lambda b, i: (b, i, 0)),
        out_shape=jax.ShapeDtypeStruct((bsz, nqb * TQ, 256), jnp.bfloat16),
        scratch_shapes=_flash_scratch(2 * B_HEADS),
        compiler_params=pltpu.CompilerParams(dimension_semantics=("arbitrary", "arbitrary"),
                                             vmem_limit_bytes=VMEM_LIMIT),
        name="attn_b",
    )(lam4, sub, qb, kbt, vb)


def _attn_c_kernel(q_ref, kt_ref, v_ref, o_ref, m_ref, l_ref, acc_ref, *, n_lat):
    i = pl.program_id(1)
    n_lat_tiles = n_lat // TK
    is_ctx = i == n_lat // TQ
    lane2 = lax.broadcasted_iota(jnp.int32, (TQ, 2 * LANES), 1)
    even = (lane2 < 64) | ((lane2 >= 128) & (lane2 < 160))
    odd = ((lane2 >= 64) & (lane2 < 128)) | ((lane2 >= 160) & (lane2 < 192))
    qms = []
    for h in range(C_HEADS):
        qp = q_ref[0, :, 256 * (h // 2):256 * (h // 2 + 1)]
        qms.append(jnp.where(even if h % 2 == 0 else odd, qp, jnp.zeros_like(qp)))

    def kts_of(c0, n, h):
        pr = h // 2
        return [kt_ref[0, c0 + k, 256 * pr:256 * (pr + 1), :] for k in range(n)]

    def v_of(c0, n, h):
        pr = h // 2
        return v_ref[0, c0 * TK:(c0 + n) * TK, LANES * pr:LANES * (pr + 1)]

    _attend(m_ref, l_ref, acc_ref, qms, is_ctx, n_lat_tiles, kts_of, v_of)

    low_half = lax.broadcasted_iota(jnp.int32, (TQ, LANES), 1) < C_V
    for pr in range(C_HEADS // 2):
        o = jnp.where(low_half, acc_ref[2 * pr], acc_ref[2 * pr + 1])
        o_ref[0, :, LANES * pr:LANES * (pr + 1)] = o.astype(o_ref.dtype)


def _attn_c_call(qc, kct, vc, n_lat, nqb):
    bsz, t, _ = qc.shape
    nb = t // TQ
    return pl.pallas_call(
        functools.partial(_attn_c_kernel, n_lat=n_lat),
        grid=(bsz, nqb),
        in_specs=[pl.BlockSpec((1, TQ, 768), lambda b, i: (b, i, 0)),
                  _resident_spec((1, nb, 768, TK)), _resident_spec((1, t, 384))],
        out_specs=pl.BlockSpec((1, TQ, 384), lambda b, i: (b, i, 0)),
        out_shape=jax.ShapeDtypeStruct((bsz, nqb * TQ, 384), jnp.bfloat16),
        scratch_shapes=_flash_scratch(C_HEADS),
        compiler_params=pltpu.CompilerParams(dimension_semantics=("arbitrary", "arbitrary"),
                                             vmem_limit_bytes=VMEM_LIMIT),
        name="attn_c",
    )(qc, kct, vc)


def _post_kernel(x_ref, mod_ref, oa_ref, ob_ref, oc_ref, wout_ref, gpm_ref, gpf_ref, gqf_ref,
                 wg_ref, wu_ref, wd_ref, o_ref):
    mod = mod_ref[0]
    tm = x_ref.shape[1]
    nh = 2 if tm >= 2 * TQ else 1
    hs = [slice(k * (tm // nh), (k + 1) * (tm // nh)) for k in range(nh)]
    mixd = [_dot(jnp.concatenate([oa_ref[0, h], ob_ref[0, h], oc_ref[0, h]], axis=1), wout_ref[...]) for h in hs]
    xs, gs, us = [], [], []
    for k, h in enumerate(hs):
        x = x_ref[0, h] + mod[2:3] * _rms(mixd[k], gpm_ref[...])
        hb = (_rms(x, gpf_ref[...]) * (1.0 + mod[4:5]) + mod[3:4]).astype(jnp.bfloat16)
        xs.append(x)
        gs.append(_dot(hb, wg_ref[...]))
        us.append(_dot(hb, wu_ref[...]))
    ffs = []
    for k in range(nh):
        act = (gs[k] * jax.nn.sigmoid(gs[k]) * us[k]).astype(jnp.bfloat16)
        ffs.append(_dot(act, wd_ref[...]))
    for k, h in enumerate(hs):
        o_ref[0, h] = xs[k] + mod[5:6] * _rms(ffs[k], gqf_ref[...])


def _post_call(x, mods, oa, ob, oc, wout, gpm, gpf, gqf, wg, wu, wd, *, tm, blk0, mod_row):
    bsz, n, d = x.shape
    tok = lambda w: pl.BlockSpec((1, tm, w), lambda i, b: (b, i, 0))
    att = lambda w: pl.BlockSpec((1, tm, w), lambda i, b: (b, blk0 + i, 0))
    mod_spec = pl.BlockSpec((1, 6, d), lambda i, b: (2 * b + mod_row, 0, 0))
    return pl.pallas_call(
        _post_kernel,
        grid=(n // tm, bsz),
        in_specs=[tok(d), mod_spec, att(384), att(256), att(384),
                  _const_spec(wout.shape), _const_spec((1, d)), _const_spec((1, d)), _const_spec((1, d)),
                  _const_spec(wg.shape), _const_spec(wu.shape), _const_spec(wd.shape)],
        out_specs=tok(d),
        out_shape=jax.ShapeDtypeStruct((bsz, n, d), jnp.float32),
        compiler_params=pltpu.CompilerParams(dimension_semantics=("arbitrary", "arbitrary"),
                                             vmem_limit_bytes=VMEM_LIMIT),
        name="post",
    )(x, mods, oa, ob, oc, wout, gpm, gpf, gqf, wg, wu, wd)


def kernel(x, c, ctx, c_ctx, w_ada, b_ada, g_pre_mix, g_post_mix, w_in, win_sink, diff_lambda_q1, diff_lambda_k1, diff_lambda_q2, diff_lambda_k2, diff_sub_norm, mla_q_norm, mla_w_q_up, mla_kv_norm, mla_w_kv_up, w_out, g_pre_ffn, g_post_ffn, w_gate, w_up, w_down):
    bsz, n_lat, d = x.shape
    n_ctx = ctx.shape[1]
    depth = w_ada.shape[0]
    assert d == D_MODEL and n_lat % TM_POST == 0 and n_ctx == TQ and n_lat >= BAND
    n_lat_blocks = n_lat // TQ
    bf = jnp.bfloat16

    cvec = jnp.zeros((8, d), jnp.float32).at[:bsz].set(c).at[bsz].set(c_ctx)
    mod_all = _ada_call(cvec, w_ada, b_ada)
    tables = _rope_tables(n_lat, n_ctx)

    ext_cols = _ext_columns()
    qup_cols = _qup_columns()
    kvup_cols = _kvup_columns()

    for l in range(depth):
        update_ctx = l < depth - 1
        m = mod_all[l].reshape(8, 6, d)
        mods = jnp.stack([m[:bsz], jnp.broadcast_to(m[bsz], (bsz, 6, d))], axis=1).reshape(2 * bsz, 6, d)

        w_ext = jnp.concatenate([w_in[l], jnp.zeros((d, 1), jnp.float32)], axis=1)[:, ext_cols].astype(bf)
        wq = jnp.concatenate([mla_w_q_up[l], jnp.zeros((C_Q_RANK, 1), jnp.float32)], axis=1)[:, qup_cols].astype(bf)
        wkv = mla_w_kv_up[l][:, kvup_cols].astype(bf)

        qa, ka, va, qb, kbt, vb, qc, kct, vc = _proj_call(
            x, ctx, mods, g_pre_mix[l][None], w_ext, mla_q_norm[l][None], wq, mla_kv_norm[l][None], wkv, tables)

        lam_init = 0.8 - 0.6 * math.exp(-0.3 * l)
        lam4 = jnp.stack([diff_lambda_q1[l], diff_lambda_k1[l], diff_lambda_q2[l], diff_lambda_k2[l]])
        sub = jnp.concatenate([diff_sub_norm[l], diff_sub_norm[l]])[None]

        nqb = n_lat_blocks + (1 if update_ctx else 0)
        oa = _attn_a_call(win_sink[l], qa, ka, va, n_lat, nqb)
        ob = _attn_b_call(lam4, sub, qb, kbt, vb, n_lat, lam_init, nqb)
        oc = _attn_c_call(qc, kct, vc, n_lat, nqb)

        weights = (w_out[l].astype(bf), g_post_mix[l][None], g_pre_ffn[l][None], g_post_ffn[l][None],
                   w_gate[l].astype(bf), w_up[l].astype(bf), w_down[l].astype(bf))
        x_new = _post_call(x, mods, oa, ob, oc, *weights, tm=TM_POST, blk0=0, mod_row=0)
        if update_ctx:
            ctx = _post_call(ctx, mods, oa, ob, oc, *weights, tm=TQ, blk0=n_lat_blocks, mod_row=1)
        x = x_new
    return x
```

```python
import functools
import math

import numpy as np
import jax
import jax.numpy as jnp
from jax import lax
from jax.experimental import pallas as pl
from jax.experimental.pallas import tpu as pltpu

D_MODEL = 1024
GRID_W = 64
HEAD_DIM = 64
A_HEADS = 6
A_KV_HEADS = 2
A_GROUP = A_HEADS // A_KV_HEADS
WINDOW = 128
B_HEADS = 4
B_QK_DIM = 32
B_V_DIM = 64
C_HEADS = 6
C_Q_RANK = 256
C_KV_RANK = 128
C_NOPE = 64
C_ROPE = 32
C_V = 64
D_FF = 2816
ROPE_BASE = 10000.0
NORM_EPS = 1e-6
NEG_INF = -1e30
LOG2E = math.log2(math.e)
A_SCALE = HEAD_DIM ** -0.5 * LOG2E
B_SCALE = B_QK_DIM ** -0.5 * LOG2E
C_SCALE = (C_NOPE + C_ROPE) ** -0.5 * LOG2E

LANES = 128
TQ = 256
TM_POST = 512
TK = 256
SUB = 11
LOOKAHEAD = 1
GROUP = 11
ROW_SUM_LIMIT = 2.0 ** 100
BAND = TQ + 2 * WINDOW
VMEM_LIMIT = 56 * 1024 * 1024

_AQ, _AK, _AV = 0, 384, 512
_DQ, _DK, _DV = 640, 896, 1152
_MQ, _MKV, _MKR = 1408, 1664, 1792
_ZERO_COL = 1824

N_ROPED = 384 + 256 + 256 + 256 + 128
N_PLAIN = 256 + 256 + 256 + 128
N_EXT = 2 * N_ROPED + N_PLAIN


def _rope_partner(unit):
    half = unit // 2
    quarter = half // 2
    d = np.arange(unit)
    r = d % half
    partner = np.where(r < quarter, d + quarter, d - quarter)
    sign = np.where(r < quarter, -1.0, 1.0).astype(np.float32)
    return partner, sign


def _ext_columns():
    pa, _ = _rope_partner(HEAD_DIM)
    pb, _ = _rope_partner(B_QK_DIM)

    def rot(src, unit, partner):
        return (src // unit) * unit + partner[src % unit]

    qa = np.arange(A_HEADS * HEAD_DIM)
    dup = np.concatenate([np.arange(64), np.arange(64), 64 + np.arange(64), 64 + np.arange(64)])
    qb = np.arange(256)
    kr = np.concatenate([np.arange(32), np.arange(32)])
    zeros64 = np.full((64,), _ZERO_COL)
    main = [_AQ + qa, _AK + dup, _DQ + qb, _DK + qb, np.concatenate([_MKR + kr, zeros64])]
    rots = [_AQ + rot(qa, 64, pa), _AK + rot(dup, 64, pa), _DQ + rot(qb, 32, pb), _DK + rot(qb, 32, pb),
            np.concatenate([_MKR + rot(kr, 32, pb), zeros64])]
    plain = [_AV + dup, _DV + qb, _MQ + np.arange(256), _MKV + np.arange(128)]
    cols = np.concatenate(main + rots + plain)
    assert cols.shape[0] == N_EXT
    return cols


def _qup_columns():
    pb, _ = _rope_partner(C_ROPE)
    width = C_NOPE + C_ROPE
    zero = C_HEADS * width
    zeros64 = np.full((64,), zero)
    main, rots = [], []
    for p in range(C_HEADS // 2):
        h0, h1 = 2 * p, 2 * p + 1
        main += [h0 * width + np.arange(64), h1 * width + np.arange(64),
                 h0 * width + 64 + np.arange(32), h1 * width + 64 + np.arange(32), zeros64]
        rots += [h0 * width + 64 + pb, h1 * width + 64 + pb, zeros64]
    return np.concatenate(main + rots)


def _kvup_columns():
    width = C_NOPE + C_V
    k = np.concatenate([h * width + np.arange(64) for h in range(C_HEADS)])
    v = np.concatenate([h * width + 64 + np.arange(64) for h in range(C_HEADS)])
    return np.concatenate([k, v])


def _rope_tables(n_lat, n_ctx):
    t = jnp.arange(n_lat)
    row = (t // GRID_W).astype(jnp.float32)
    col = (t % GRID_W).astype(jnp.float32)

    def table(unit):
        half = unit // 2
        quarter = half // 2
        _, sign = _rope_partner(unit)
        lane = np.arange(LANES)
        d = lane % unit
        axis = d // half
        j = d % quarter
        freqs = ROPE_BASE ** (-jnp.arange(0, half, 2, dtype=jnp.float32) / half)
        f = freqs[j]
        pos = jnp.where(jnp.asarray(axis)[None, :] == 0, row[:, None], col[:, None])
        ang = pos * f[None, :]
        cos = jnp.cos(ang)
        sin = jnp.sin(ang) * jnp.asarray(sign[d])[None, :]
        cos = jnp.concatenate([cos, jnp.ones((n_ctx, LANES), jnp.float32)], axis=0)
        sin = jnp.concatenate([sin, jnp.zeros((n_ctx, LANES), jnp.float32)], axis=0)
        return cos, sin

    cos_a, sin_a = table(HEAD_DIM)
    cos_b, sin_b = table(B_QK_DIM)
    return cos_a, sin_a, cos_b, sin_b


def _rms(x, g):
    return x * lax.rsqrt(jnp.mean(x * x, axis=-1, keepdims=True) + NORM_EPS) * g


def _dot(a, b):
    return jnp.dot(a, b, preferred_element_type=jnp.float32)


def _dot_nt(a, b):
    return lax.dot_general(a, b, (((1,), (1,)), ((), ())), preferred_element_type=jnp.float32)


def _ada_kernel(c_ref, w_ref, b_ref, o_ref):
    c = c_ref[...]
    h = c * jax.nn.sigmoid(c)
    o_ref[0] = _dot(h, w_ref[0]) + b_ref[0]


def _ada_call(cvec, w_ada, b_ada):
    depth, d, n = w_ada.shape
    tn = 1536
    return pl.pallas_call(
        _ada_kernel,
        grid=(depth, n // tn),
        in_specs=[pl.BlockSpec((8, d), lambda l, j: (0, 0)),
                  pl.BlockSpec((1, d, tn), lambda l, j: (l, 0, j)),
                  pl.BlockSpec((1, 1, tn), lambda l, j: (l, 0, j))],
        out_specs=pl.BlockSpec((1, 8, tn), lambda l, j: (l, 0, j)),
        out_shape=jax.ShapeDtypeStruct((depth, 8, n), jnp.float32),
        compiler_params=pltpu.CompilerParams(dimension_semantics=("arbitrary", "arbitrary"),
                                             vmem_limit_bytes=VMEM_LIMIT),
        name="adaln",
    )(cvec, w_ada, b_ada.reshape(depth, 1, n))


def _proj_kernel(x_ref, ctx_ref, mod_ref, gpre_ref, w_ref, qn_ref, wq_ref, kvn_ref, wkv_ref,
                 ca_ref, sa_ref, cb_ref, sb_ref,
                 qa_ref, ka_ref, va_ref, qb_ref, kbt_ref, vb_ref, qc_ref, kct_ref, vc_ref, *, n_lat_blocks):
    mod = mod_ref[0]
    x = jnp.where(pl.program_id(1) == n_lat_blocks, ctx_ref[0], x_ref[0])
    h = _rms(x, gpre_ref[...]) * (1.0 + mod[1:2]) + mod[0:1]
    p = _dot(h.astype(jnp.bfloat16), w_ref[...])

    ca, sa, cb, sb = ca_ref[...], sa_ref[...], cb_ref[...], sb_ref[...]

    def roped(off, width, cos, sin):
        n = width // LANES
        main = p[:, off:off + width]
        rot = p[:, N_ROPED + off:N_ROPED + off + width]
        return main * jnp.concatenate([cos] * n, axis=1) + rot * jnp.concatenate([sin] * n, axis=1)

    qa_ref[0] = (roped(0, 384, ca, sa) * A_SCALE).astype(jnp.bfloat16)
    ka_ref[0] = roped(384, 256, ca, sa).astype(jnp.bfloat16)
    qb_ref[0] = (roped(640, 256, cb, sb) * B_SCALE).astype(jnp.bfloat16)
    kbt_ref[0, 0] = roped(896, 256, cb, sb).T.astype(jnp.bfloat16)
    kr = roped(1152, 128, cb, sb)

    o = 2 * N_ROPED
    va_ref[0] = p[:, o:o + 256].astype(jnp.bfloat16)
    vb_ref[0] = p[:, o + 256:o + 512].astype(jnp.bfloat16)
    cq = p[:, o + 512:o + 768]
    ckv = p[:, o + 768:o + 896]

    qc = _dot(_rms(cq, qn_ref[...]).astype(jnp.bfloat16), wq_ref[...])
    kv = _dot(_rms(ckv, kvn_ref[...]).astype(jnp.bfloat16), wkv_ref[...])
    vc_ref[0] = kv[:, 384:].astype(jnp.bfloat16)
    kr_t = kr.T.astype(jnp.bfloat16)
    for pr in range(C_HEADS // 2):
        nope = qc[:, 256 * pr:256 * pr + 128]
        rope = (qc[:, 256 * pr + 128:256 * pr + 256] * cb
                + qc[:, 768 + 128 * pr:768 + 128 * pr + 128] * sb)
        qc_ref[0, :, 256 * pr:256 * pr + 128] = (nope * C_SCALE).astype(jnp.bfloat16)
        qc_ref[0, :, 256 * pr + 128:256 * pr + 256] = (rope * C_SCALE).astype(jnp.bfloat16)
        kct_ref[0, 0, 256 * pr:256 * pr + 128, :] = kv[:, 128 * pr:128 * pr + 128].T.astype(jnp.bfloat16)
        kct_ref[0, 0, 256 * pr + 128:256 * pr + 256, :] = kr_t


def _const_spec(shape):
    nd = len(shape)
    return pl.BlockSpec(shape, lambda *_: (0,) * nd, pipeline_mode=pl.Buffered(1))


def _proj_call(x, ctx, mods, gpre, w_ext, qn, wq, kvn, wkv, tables):
    bsz, n_lat, d = x.shape
    n_lat_blocks = n_lat // TQ
    nb = n_lat_blocks + 1
    t = nb * TQ
    tok = lambda w: pl.BlockSpec((1, TQ, w), lambda b, i: (b, i, 0))
    tbl = pl.BlockSpec((TQ, LANES), lambda b, i: (i, 0))
    kt = lambda r: pl.BlockSpec((1, 1, r, TQ), lambda b, i: (b, i, 0, 0))
    x_spec = pl.BlockSpec((1, TQ, d), lambda b, i: (b, jnp.minimum(i, n_lat_blocks - 1), 0))
    ctx_spec = pl.BlockSpec((1, TQ, d), lambda b, i: (b, 0, 0))
    mod_spec = pl.BlockSpec((1, 6, d), lambda b, i: (2 * b + (i >= n_lat_blocks).astype(jnp.int32), 0, 0))
    bf = jnp.bfloat16
    out_shape = [jax.ShapeDtypeStruct((bsz, t, 384), bf), jax.ShapeDtypeStruct((bsz, t, 256), bf),
                 jax.ShapeDtypeStruct((bsz, t, 256), bf), jax.ShapeDtypeStruct((bsz, t, 256), bf),
                 jax.ShapeDtypeStruct((bsz, nb, 256, TQ), bf), jax.ShapeDtypeStruct((bsz, t, 256), bf),
                 jax.ShapeDtypeStruct((bsz, t, 768), bf), jax.ShapeDtypeStruct((bsz, nb, 768, TQ), bf),
                 jax.ShapeDtypeStruct((bsz, t, 384), bf)]
    out_specs = [tok(384), tok(256), tok(256), tok(256), kt(256), tok(256), tok(768), kt(768), tok(384)]
    return pl.pallas_call(
        functools.partial(_proj_kernel, n_lat_blocks=n_lat_blocks),
        grid=(bsz, nb),
        in_specs=[x_spec, ctx_spec, mod_spec, _const_spec((1, d)), _const_spec(w_ext.shape),
                  _const_spec(qn.shape), _const_spec(wq.shape), _const_spec(kvn.shape), _const_spec(wkv.shape),
                  tbl, tbl, tbl, tbl],
        out_specs=out_specs,
        out_shape=out_shape,
        compiler_params=pltpu.CompilerParams(dimension_semantics=("arbitrary", "arbitrary"),
                                             vmem_limit_bytes=VMEM_LIMIT),
        name="proj",
    )(x, ctx, mods, gpre, w_ext, qn, wq, kvn, wkv, *tables)


def _attn_a_kernel(sink_ref, q_ref, k_ref, v_ref, o_ref, bias_ref, *, n_lat):
    i = pl.program_id(1)
    is_ctx = i == n_lat // TQ
    t0 = i * TQ
    start = pl.multiple_of(jnp.clip(t0 - WINDOW, 0, n_lat - BAND), WINDOW)
    kcat = jnp.concatenate([k_ref[0, pl.ds(start, BAND), :], k_ref[0, n_lat:, :]], axis=0)
    vcat = jnp.concatenate([v_ref[0, pl.ds(start, BAND), :], v_ref[0, n_lat:, :]], axis=0)
    qpos = t0 + jnp.where(is_ctx, 4 * n_lat, 0) + lax.broadcasted_iota(jnp.int32, (TQ, BAND), 0)
    kpos = start + lax.broadcasted_iota(jnp.int32, (TQ, BAND), 1)
    bias_ref[...] = jnp.where(jnp.abs(qpos - kpos) <= WINDOW, 0.0, NEG_INF)
    low_half = lax.broadcasted_iota(jnp.int32, (TQ, LANES), 1) < HEAD_DIM

    def scores(head):
        qg = q_ref[0, :, LANES * (head // 2):LANES * (head // 2 + 1)]
        kv = head // A_GROUP
        qm = jnp.where(low_half if head % 2 == 0 else jnp.logical_not(low_half), qg, jnp.zeros_like(qg))
        return _dot_nt(qm, kcat[:, LANES * kv:LANES * (kv + 1)])

    outs = []
    nxt = scores(0)
    for head in range(A_HEADS):
        s = nxt
        if head + 1 < A_HEADS:
            nxt = scores(head + 1)
        kv = head // A_GROUP
        n_parts = s.shape[1] // LANES
        parts = [s[:, k * LANES:(k + 1) * LANES] for k in range(n_parts)]
        parts = [part + bias_ref[:, k * LANES:(k + 1) * LANES] if k < BAND // LANES else part
                 for k, part in enumerate(parts)]
        sink = sink_ref[head] * LOG2E
        m = jnp.maximum(jnp.max(functools.reduce(jnp.maximum, parts), axis=1, keepdims=True), sink)
        ps = [jnp.exp2(part - m) for part in parts]
        l = jnp.sum(functools.reduce(jnp.add, ps), axis=1, keepdims=True) + jnp.exp2(sink - m)
        pb = jnp.concatenate([p.astype(jnp.bfloat16) for p in ps], axis=1)
        outs.append(_dot(pb, vcat[:, LANES * kv:LANES * (kv + 1)]) / l)
        if head % 2 == 1:
            pr = head // 2
            o_ref[0, :, LANES * pr:LANES * (pr + 1)] = jnp.where(low_half, outs[-2], outs[-1]).astype(o_ref.dtype)


def _attn_a_call(sink, qa, ka, va, n_lat, nqb):
    bsz, t, _ = qa.shape
    return pl.pallas_call(
        functools.partial(_attn_a_kernel, n_lat=n_lat),
        grid=(bsz, nqb),
        in_specs=[pl.BlockSpec(memory_space=pltpu.SMEM),
                  pl.BlockSpec((1, TQ, 384), lambda b, i: (b, i, 0)),
                  pl.BlockSpec((1, t, 256), lambda b, i: (b, 0, 0)),
                  pl.BlockSpec((1, t, 256), lambda b, i: (b, 0, 0))],
        out_specs=pl.BlockSpec((1, TQ, 384), lambda b, i: (b, i, 0)),
        out_shape=jax.ShapeDtypeStruct((bsz, nqb * TQ, 384), jnp.bfloat16),
        scratch_shapes=[pltpu.VMEM((TQ, BAND), jnp.float32)],
        compiler_params=pltpu.CompilerParams(dimension_semantics=("arbitrary", "arbitrary"),
                                             vmem_limit_bytes=VMEM_LIMIT),
        name="attn_a",
    )(sink, qa, ka, va)


def _flash_init(m_ref, l_ref, acc_ref):
    m_ref[...] = jnp.full(m_ref.shape, NEG_INF, jnp.float32)
    l_ref[...] = jnp.zeros(l_ref.shape, jnp.float32)
    acc_ref[...] = jnp.zeros(acc_ref.shape, jnp.float32)


def _flash_scores(qm, kts):
    parts = []
    for kt in kts:
        s = _dot(qm, kt)
        parts += [s[:, k * LANES:(k + 1) * LANES] for k in range(TK // LANES)]
    return parts


def _flash_update(j, m_ref, l_ref, acc_ref, parts, v):
    m_old = m_ref[j]
    m_new = jnp.maximum(m_old, jnp.max(functools.reduce(jnp.maximum, parts), axis=1, keepdims=True))
    alpha = jnp.exp2(m_old - m_new)
    ps = [jnp.exp2(part - m_new) for part in parts]
    l_ref[j] = alpha * l_ref[j] + functools.reduce(jnp.add, ps)
    pb = jnp.concatenate([p.astype(jnp.bfloat16) for p in ps], axis=1)
    acc_ref[j] = alpha * acc_ref[j] + _dot(pb, v)
    m_ref[j] = m_new


def _key_steps(n_tiles, size):
    return [(c, min(size, n_tiles - c)) for c in range(0, n_tiles, size)]


def _flash_run(m_ref, l_ref, acc_ref, qms, steps, kts_of, v_of):
    seq = [(c0, n, j) for (c0, n) in steps for j in range(len(qms))]
    scores = lambda item: _flash_scores(qms[item[2]], kts_of(*item))
    pending = [scores(item) for item in seq[:LOOKAHEAD]]
    for idx, (c0, n, j) in enumerate(seq):
        if idx + LOOKAHEAD < len(seq):
            pending.append(scores(seq[idx + LOOKAHEAD]))
        _flash_update(j, m_ref, l_ref, acc_ref, pending.pop(0), v_of(c0, n, j))


def _single_pass_run(acc_ref, qms, groups, kts_of, v_of):
    seq = [(j, gi) for j in range(len(qms)) for gi in range(len(groups))]
    scores = lambda item: _flash_scores(qms[item[0]], kts_of(*groups[item[1]], item[0]))
    pending = [scores(seq[0])]
    l_max = None
    for idx, (j, gi) in enumerate(seq):
        if idx + 1 < len(seq):
            pending.append(scores(seq[idx + 1]))
        parts = pending.pop(0)
        if gi == 0:
            shift = jnp.max(functools.reduce(jnp.maximum, parts[:TK // LANES]), axis=1, keepdims=True)
            l = acc = None
        ps = [jnp.exp2(part - shift) for part in parts]
        l_g = functools.reduce(jnp.add, ps)
        pb = jnp.concatenate([p.astype(jnp.bfloat16) for p in ps], axis=1)
        pv = _dot(pb, v_of(*groups[gi], j))
        l, acc = (l_g, pv) if gi == 0 else (l + l_g, acc + pv)
        if gi == len(groups) - 1:
            l_row = jnp.sum(l, axis=1, keepdims=True)
            acc_ref[j] = acc / l_row
            l_max = jnp.max(l_row) if l_max is None else jnp.maximum(l_max, jnp.max(l_row))
    return l_max


def _attend(m_ref, l_ref, acc_ref, qms, is_ctx, n_lat_tiles, kts_of, v_of):
    @pl.when(is_ctx)
    def _():
        _single_pass_run(acc_ref, qms, [(n_lat_tiles, 1)], kts_of, v_of)

    @pl.when(jnp.logical_not(is_ctx))
    def _():
        l_max = _single_pass_run(acc_ref, qms, _key_steps(n_lat_tiles + 1, GROUP), kts_of, v_of)

        @pl.when(jnp.logical_not(l_max < ROW_SUM_LIMIT))
        def _():
            _flash_init(m_ref, l_ref, acc_ref)
            _flash_run(m_ref, l_ref, acc_ref, qms, _key_steps(n_lat_tiles + 1, SUB), kts_of, v_of)
            for j in range(len(qms)):
                acc_ref[j] = acc_ref[j] / jnp.sum(l_ref[j], axis=1, keepdims=True)


def _resident_spec(shape):
    nd = len(shape)
    return pl.BlockSpec(shape, lambda b, i: (b,) + (0,) * (nd - 1), pipeline_mode=pl.Buffered(1))


def _flash_scratch(n_maps):
    return [pltpu.VMEM((n_maps, TQ, LANES), jnp.float32) for _ in range(3)]


def _attn_b_kernel(lam_ref, sub_ref, q_ref, kt_ref, v_ref, o_ref, m_ref, l_ref, acc_ref, *, n_lat, lam_init):
    i = pl.program_id(1)
    n_lat_tiles = n_lat // TK
    is_ctx = i == n_lat // TQ
    lam4 = lam_ref[...]
    lam = (jnp.exp(jnp.sum(lam4[0:1] * lam4[1:2], axis=-1, keepdims=True))
           - jnp.exp(jnp.sum(lam4[2:3] * lam4[3:4], axis=-1, keepdims=True)) + lam_init)

    lane = lax.broadcasted_iota(jnp.int32, (TQ, LANES), 1)
    n_maps = 2 * B_HEADS
    qms = []
    for j in range(n_maps):
        qg = q_ref[0, :, LANES * (j // 4):LANES * (j // 4 + 1)]
        qms.append(jnp.where((lane // B_QK_DIM) == (j % 4), qg, jnp.zeros_like(qg)))

    def kts_of(c0, n, j):
        g = j // 4
        return [kt_ref[0, c0 + k, LANES * g:LANES * (g + 1), :] for k in range(n)]

    def v_of(c0, n, j):
        g = j // 4
        return v_ref[0, c0 * TK:(c0 + n) * TK, LANES * g:LANES * (g + 1)]

    _attend(m_ref, l_ref, acc_ref, qms, is_ctx, n_lat_tiles, kts_of, v_of)

    low_half = lane < B_V_DIM
    for g in range(2):
        heads = []
        for e in range(2):
            heads.append(acc_ref[4 * g + 2 * e] - lam * acc_ref[4 * g + 2 * e + 1])
        o = jnp.where(low_half, heads[0], heads[1])
        sq = o * o
        ms_lo = jnp.sum(jnp.where(low_half, sq, 0.0), axis=-1, keepdims=True) * (1.0 / B_V_DIM)
        ms_hi = jnp.sum(jnp.where(low_half, 0.0, sq), axis=-1, keepdims=True) * (1.0 / B_V_DIM)
        ms = jnp.where(low_half, ms_lo, ms_hi)
        y = o * lax.rsqrt(ms + NORM_EPS) * sub_ref[...] * (1.0 - lam_init)
        o_ref[0, :, LANES * g:LANES * (g + 1)] = y.astype(o_ref.dtype)


def _attn_b_call(lam4, sub, qb, kbt, vb, n_lat, lam_init, nqb):
    bsz, t, _ = qb.shape
    nb = t // TQ
    return pl.pallas_call(
        functools.partial(_attn_b_kernel, n_lat=n_lat, lam_init=lam_init),
        grid=(bsz, nqb),
        in_specs=[pl.BlockSpec((4, B_QK_DIM), lambda b, i: (0, 0)),
                  pl.BlockSpec((1, LANES), lambda b, i: (0, 0)),
                  pl.BlockSpec((1, TQ, 256), lambda b, i: (b, i, 0)),
                  _resident_spec((1, nb, 256, TK)), _resident_spec((1, t, 256))],
        out_specs=pl.BlockSpec((1, TQ, 256), lambda b, i: (b, i, 0)),
        out_shape=jax.ShapeDtypeStruct((bsz, nqb * TQ, 256), jnp.bfloat16),
        scratch_shapes=_flash_scratch(2 * B_HEADS),
        compiler_params=pltpu.CompilerParams(dimension_semantics=("arbitrary", "arbitrary"),
                                             vmem_limit_bytes=VMEM_LIMIT),
        name="attn_b",
    )(lam4, sub, qb, kbt, vb)


def _attn_c_kernel(q_ref, kt_ref, v_ref, o_ref, m_ref, l_ref, acc_ref, *, n_lat):
    i = pl.program_id(1)
    n_lat_tiles = n_lat // TK
    is_ctx = i == n_lat // TQ
    lane2 = lax.broadcasted_iota(jnp.int32, (TQ, 2 * LANES), 1)
    even = (lane2 < 64) | ((lane2 >= 128) & (lane2 < 160))
    odd = ((lane2 >= 64) & (lane2 < 128)) | ((lane2 >= 160) & (lane2 < 192))
    qms = []
    for h in range(C_HEADS):
        qp = q_ref[0, :, 256 * (h // 2):256 * (h // 2 + 1)]
        qms.append(jnp.where(even if h % 2 == 0 else odd, qp, jnp.zeros_like(qp)))

    def kts_of(c0, n, h):
        pr = h // 2
        return [kt_ref[0, c0 + k, 256 * pr:256 * (pr + 1), :] for k in range(n)]

    def v_of(c0, n, h):
        pr = h // 2
        return v_ref[0, c0 * TK:(c0 + n) * TK, LANES * pr:LANES * (pr + 1)]

    _attend(m_ref, l_ref, acc_ref, qms, is_ctx, n_lat_tiles, kts_of, v_of)

    low_half = lax.broadcasted_iota(jnp.int32, (TQ, LANES), 1) < C_V
    for pr in range(C_HEADS // 2):
        o = jnp.where(low_half, acc_ref[2 * pr], acc_ref[2 * pr + 1])
        o_ref[0, :, LANES * pr:LANES * (pr + 1)] = o.astype(o_ref.dtype)


def _attn_c_call(qc, kct, vc, n_lat, nqb):
    bsz, t, _ = qc.shape
    nb = t // TQ
    return pl.pallas_call(
        functools.partial(_attn_c_kernel, n_lat=n_lat),
        grid=(bsz, nqb),
        in_specs=[pl.BlockSpec((1, TQ, 768), lambda b, i: (b, i, 0)),
                  _resident_spec((1, nb, 768, TK)), _resident_spec((1, t, 384))],
        out_specs=pl.BlockSpec((1, TQ, 384), lambda b, i: (b, i, 0)),
        out_shape=jax.ShapeDtypeStruct((bsz, nqb * TQ, 384), jnp.bfloat16),
        scratch_shapes=_flash_scratch(C_HEADS),
        compiler_params=pltpu.CompilerParams(dimension_semantics=("arbitrary", "arbitrary"),
                                             vmem_limit_bytes=VMEM_LIMIT),
        name="attn_c",
    )(qc, kct, vc)


def _post_kernel(x_ref, mod_ref, oa_ref, ob_ref, oc_ref, wout_ref, gpm_ref, gpf_ref, gqf_ref,
                 wg_ref, wu_ref, wd_ref, o_ref):
    mod = mod_ref[0]
    tm = x_ref.shape[1]
    nh = 2 if tm >= 2 * TQ else 1
    hs = [slice(k * (tm // nh), (k + 1) * (tm // nh)) for k in range(nh)]
    mixd = [_dot(jnp.concatenate([oa_ref[0, h], ob_ref[0, h], oc_ref[0, h]], axis=1), wout_ref[...]) for h in hs]
    xs, gs, us = [], [], []
    for k, h in enumerate(hs):
        x = x_ref[0, h] + mod[2:3] * _rms(mixd[k], gpm_ref[...])
        hb = (_rms(x, gpf_ref[...]) * (1.0 + mod[4:5]) + mod[3:4]).astype(jnp.bfloat16)
        xs.append(x)
        gs.append(_dot(hb, wg_ref[...]))
        us.append(_dot(hb, wu_ref[...]))
    ffs = []
    for k in range(nh):
        act = (gs[k] * jax.nn.sigmoid(gs[k]) * us[k]).astype(jnp.bfloat16)
        ffs.append(_dot(act, wd_ref[...]))
    for k, h in enumerate(hs):
        o_ref[0, h] = xs[k] + mod[5:6] * _rms(ffs[k], gqf_ref[...])


def _post_call(x, mods, oa, ob, oc, wout, gpm, gpf, gqf, wg, wu, wd, *, tm, blk0, mod_row):
    bsz, n, d = x.shape
    tok = lambda w: pl.BlockSpec((1, tm, w), lambda i, b: (b, i, 0))
    att = lambda w: pl.BlockSpec((1, tm, w), lambda i, b: (b, blk0 + i, 0))
    mod_spec = pl.BlockSpec((1, 6, d), lambda i, b: (2 * b + mod_row, 0, 0))
    return pl.pallas_call(
        _post_kernel,
        grid=(n // tm, bsz),
        in_specs=[tok(d), mod_spec, att(384), att(256), att(384),
                  _const_spec(wout.shape), _const_spec((1, d)), _const_spec((1, d)), _const_spec((1, d)),
                  _const_spec(wg.shape), _const_spec(wu.shape), _const_spec(wd.shape)],
        out_specs=tok(d),
        out_shape=jax.ShapeDtypeStruct((bsz, n, d), jnp.float32),
        compiler_params=pltpu.CompilerParams(dimension_semantics=("arbitrary", "arbitrary"),
                                             vmem_limit_bytes=VMEM_LIMIT),
        name="post",
    )(x, mods, oa, ob, oc, wout, gpm, gpf, gqf, wg, wu, wd)


def kernel(x, c, ctx, c_ctx, w_ada, b_ada, g_pre_mix, g_post_mix, w_in, win_sink, diff_lambda_q1, diff_lambda_k1, diff_lambda_q2, diff_lambda_k2, diff_sub_norm, mla_q_norm, mla_w_q_up, mla_kv_norm, mla_w_kv_up, w_out, g_pre_ffn, g_post_ffn, w_gate, w_up, w_down):
    bsz, n_lat, d = x.shape
    n_ctx = ctx.shape[1]
    depth = w_ada.shape[0]
    assert d == D_MODEL and n_lat % TM_POST == 0 and n_ctx == TQ and n_lat >= BAND
    n_lat_blocks = n_lat // TQ
    bf = jnp.bfloat16

    cvec = jnp.zeros((8, d), jnp.float32).at[:bsz].set(c).at[bsz].set(c_ctx)
    mod_all = _ada_call(cvec, w_ada, b_ada)
    tables = _rope_tables(n_lat, n_ctx)

    ext_cols = _ext_columns()
    qup_cols = _qup_columns()
    kvup_cols = _kvup_columns()

    for l in range(depth):
        update_ctx = l < depth - 1
        m = mod_all[l].reshape(8, 6, d)
        mods = jnp.stack([m[:bsz], jnp.broadcast_to(m[bsz], (bsz, 6, d))], axis=1).reshape(2 * bsz, 6, d)

        w_ext = jnp.concatenate([w_in[l], jnp.zeros((d, 1), jnp.float32)], axis=1)[:, ext_cols].astype(bf)
        wq = jnp.concatenate([mla_w_q_up[l], jnp.zeros((C_Q_RANK, 1), jnp.float32)], axis=1)[:, qup_cols].astype(bf)
        wkv = mla_w_kv_up[l][:, kvup_cols].astype(bf)

        qa, ka, va, qb, kbt, vb, qc, kct, vc = _proj_call(
            x, ctx, mods, g_pre_mix[l][None], w_ext, mla_q_norm[l][None], wq, mla_kv_norm[l][None], wkv, tables)

        lam_init = 0.8 - 0.6 * math.exp(-0.3 * l)
        lam4 = jnp.stack([diff_lambda_q1[l], diff_lambda_k1[l], diff_lambda_q2[l], diff_lambda_k2[l]])
        sub = jnp.concatenate([diff_sub_norm[l], diff_sub_norm[l]])[None]

        nqb = n_lat_blocks + (1 if update_ctx else 0)
        oa = _attn_a_call(win_sink[l], qa, ka, va, n_lat, nqb)
        ob = _attn_b_call(lam4, sub, qb, kbt, vb, n_lat, lam_init, nqb)
        oc = _attn_c_call(qc, kct, vc, n_lat, nqb)

        weights = (w_out[l].astype(bf), g_post_mix[l][None], g_pre_ffn[l][None], g_post_ffn[l][None],
                   w_gate[l].astype(bf), w_up[l].astype(bf), w_down[l].astype(bf))
        x_new = _post_call(x, mods, oa, ob, oc, *weights, tm=TM_POST, blk0=0, mod_row=0)
        if update_ctx:
            ctx = _post_call(ctx, mods, oa, ob, oc, *weights, tm=TQ, blk0=n_lat_blocks, mod_row=1)
        x = x_new
    return x
```

```python
import functools
import math

import numpy as np
import jax
import jax.numpy as jnp
from jax import lax
from jax.experimental import pallas as pl
from jax.experimental.pallas import tpu as pltpu

D_MODEL = 1024
GRID_W = 64
HEAD_DIM = 64
A_HEADS = 6
A_KV_HEADS = 2
A_GROUP = A_HEADS // A_KV_HEADS
WINDOW = 128
B_HEADS = 4
B_QK_DIM = 32
B_V_DIM = 64
C_HEADS = 6
C_Q_RANK = 256
C_KV_RANK = 128
C_NOPE = 64
C_ROPE = 32
C_V = 64
D_FF = 2816
ROPE_BASE = 10000.0
NORM_EPS = 1e-6
NEG_INF = -1e30
LOG2E = math.log2(math.e)
A_SCALE = HEAD_DIM ** -0.5 * LOG2E
B_SCALE = B_QK_DIM ** -0.5 * LOG2E
C_SCALE = (C_NOPE + C_ROPE) ** -0.5 * LOG2E

LANES = 128
TQ = 256
TM_POST = 512
TK = 256
SUB = 11
LOOKAHEAD = 1
GROUP = 11
ROW_SUM_LIMIT = 2.0 ** 100
BAND = TQ + 2 * WINDOW
VMEM_LIMIT = 56 * 1024 * 1024

_AQ, _AK, _AV = 0, 384, 512
_DQ, _DK, _DV = 640, 896, 1152
_MQ, _MKV, _MKR = 1408, 1664, 1792
_ZERO_COL = 1824

N_ROPED = 384 + 256 + 256 + 256 + 128
N_PLAIN = 256 + 256 + 256 + 128
N_EXT = 2 * N_ROPED + N_PLAIN


def _rope_partner(unit):
    half = unit // 2
    quarter = half // 2
    d = np.arange(unit)
    r = d % half
    partner = np.where(r < quarter, d + quarter, d - quarter)
    sign = np.where(r < quarter, -1.0, 1.0).astype(np.float32)
    return partner, sign


def _ext_columns():
    pa, _ = _rope_partner(HEAD_DIM)
    pb, _ = _rope_partner(B_QK_DIM)

    def rot(src, unit, partner):
        return (src // unit) * unit + partner[src % unit]

    qa = np.arange(A_HEADS * HEAD_DIM)
    dup = np.concatenate([np.arange(64), np.arange(64), 64 + np.arange(64), 64 + np.arange(64)])
    qb = np.arange(256)
    kr = np.concatenate([np.arange(32), np.arange(32)])
    zeros64 = np.full((64,), _ZERO_COL)
    main = [_AQ + qa, _AK + dup, _DQ + qb, _DK + qb, np.concatenate([_MKR + kr, zeros64])]
    rots = [_AQ + rot(qa, 64, pa), _AK + rot(dup, 64, pa), _DQ + rot(qb, 32, pb), _DK + rot(qb, 32, pb),
            np.concatenate([_MKR + rot(kr, 32, pb), zeros64])]
    plain = [_AV + dup, _DV + qb, _MQ + np.arange(256), _MKV + np.arange(128)]
    cols = np.concatenate(main + rots + plain)
    assert cols.shape[0] == N_EXT
    return cols


def _qup_columns():
    pb, _ = _rope_partner(C_ROPE)
    width = C_NOPE + C_ROPE
    zero = C_HEADS * width
    zeros64 = np.full((64,), zero)
    main, rots = [], []
    for p in range(C_HEADS // 2):
        h0, h1 = 2 * p, 2 * p + 1
        main += [h0 * width + np.arange(64), h1 * width + np.arange(64),
                 h0 * width + 64 + np.arange(32), h1 * width + 64 + np.arange(32), zeros64]
        rots += [h0 * width + 64 + pb, h1 * width + 64 + pb, zeros64]
    return np.concatenate(main + rots)


def _kvup_columns():
    width = C_NOPE + C_V
    k = np.concatenate([h * width + np.arange(64) for h in range(C_HEADS)])
    v = np.concatenate([h * width + 64 + np.arange(64) for h in range(C_HEADS)])
    return np.concatenate([k, v])


def _rope_tables(n_lat, n_ctx):
    t = jnp.arange(n_lat)
    row = (t // GRID_W).astype(jnp.float32)
    col = (t % GRID_W).astype(jnp.float32)

    def table(unit):
        half = unit // 2
        quarter = half // 2
        _, sign = _rope_partner(unit)
        lane = np.arange(LANES)
        d = lane % unit
        axis = d // half
        j = d % quarter
        freqs = ROPE_BASE ** (-jnp.arange(0, half, 2, dtype=jnp.float32) / half)
        f = freqs[j]
        pos = jnp.where(jnp.asarray(axis)[None, :] == 0, row[:, None], col[:, None])
        ang = pos * f[None, :]
        cos = jnp.cos(ang)
        sin = jnp.sin(ang) * jnp.asarray(sign[d])[None, :]
        cos = jnp.concatenate([cos, jnp.ones((n_ctx, LANES), jnp.float32)], axis=0)
        sin = jnp.concatenate([sin, jnp.zeros((n_ctx, LANES), jnp.float32)], axis=0)
        return cos, sin

    cos_a, sin_a = table(HEAD_DIM)
    cos_b, sin_b = table(B_QK_DIM)
    return cos_a, sin_a, cos_b, sin_b


def _rms(x, g):
    return x * lax.rsqrt(jnp.mean(x * x, axis=-1, keepdims=True) + NORM_EPS) * g


def _dot(a, b):
    return jnp.dot(a, b, preferred_element_type=jnp.float32)


def _dot_nt(a, b):
    return lax.dot_general(a, b, (((1,), (1,)), ((), ())), preferred_element_type=jnp.float32)


def _ada_kernel(c_ref, w_ref, b_ref, o_ref):
    c = c_ref[...]
    h = c * jax.nn.sigmoid(c)
    o_ref[0] = _dot(h, w_ref[0]) + b_ref[0]


def _ada_call(cvec, w_ada, b_ada):
    depth, d, n = w_ada.shape
    tn = 1536
    return pl.pallas_call(
        _ada_kernel,
        grid=(depth, n // tn),
        in_specs=[pl.BlockSpec((8, d), lambda l, j: (0, 0)),
                  pl.BlockSpec((1, d, tn), lambda l, j: (l, 0, j)),
                  pl.BlockSpec((1, 1, tn), lambda l, j: (l, 0, j))],
        out_specs=pl.BlockSpec((1, 8, tn), lambda l, j: (l, 0, j)),
        out_shape=jax.ShapeDtypeStruct((depth, 8, n), jnp.float32),
        compiler_params=pltpu.CompilerParams(dimension_semantics=("arbitrary", "arbitrary"),
                                             vmem_limit_bytes=VMEM_LIMIT),
        name="adaln",
    )(cvec, w_ada, b_ada.reshape(depth, 1, n))


def _proj_kernel(x_ref, ctx_ref, mod_ref, gpre_ref, w_ref, qn_ref, wq_ref, kvn_ref, wkv_ref,
                 ca_ref, sa_ref, cb_ref, sb_ref,
                 qa_ref, ka_ref, va_ref, qb_ref, kbt_ref, vb_ref, qc_ref, kct_ref, vc_ref, *, n_lat_blocks):
    mod = mod_ref[0]
    x = jnp.where(pl.program_id(1) == n_lat_blocks, ctx_ref[0], x_ref[0])
    h = _rms(x, gpre_ref[...]) * (1.0 + mod[1:2]) + mod[0:1]
    p = _dot(h.astype(jnp.bfloat16), w_ref[...])

    ca, sa, cb, sb = ca_ref[...], sa_ref[...], cb_ref[...], sb_ref[...]

    def roped(off, width, cos, sin):
        n = width // LANES
        main = p[:, off:off + width]
        rot = p[:, N_ROPED + off:N_ROPED + off + width]
        return main * jnp.concatenate([cos] * n, axis=1) + rot * jnp.concatenate([sin] * n, axis=1)

    qa_ref[0] = (roped(0, 384, ca, sa) * A_SCALE).astype(jnp.bfloat16)
    ka_ref[0] = roped(384, 256, ca, sa).astype(jnp.bfloat16)
    qb_ref[0] = (roped(640, 256, cb, sb) * B_SCALE).astype(jnp.bfloat16)
    kbt_ref[0, 0] = roped(896, 256, cb, sb).T.astype(jnp.bfloat16)
    kr = roped(1152, 128, cb, sb)

    o = 2 * N_ROPED
    va_ref[0] = p[:, o:o + 256].astype(jnp.bfloat16)
    vb_ref[0] = p[:, o + 256:o + 512].astype(jnp.bfloat16)
    cq = p[:, o + 512:o + 768]
    ckv = p[:, o + 768:o + 896]

    qc = _dot(_rms(cq, qn_ref[...]).astype(jnp.bfloat16), wq_ref[...])
    kv = _dot(_rms(ckv, kvn_ref[...]).astype(jnp.bfloat16), wkv_ref[...])
    vc_ref[0] = kv[:, 384:].astype(jnp.bfloat16)
    kr_t = kr.T.astype(jnp.bfloat16)
    for pr in range(C_HEADS // 2):
        nope = qc[:, 256 * pr:256 * pr + 128]
        rope = (qc[:, 256 * pr + 128:256 * pr + 256] * cb
                + qc[:, 768 + 128 * pr:768 + 128 * pr + 128] * sb)
        qc_ref[0, :, 256 * pr:256 * pr + 128] = (nope * C_SCALE).astype(jnp.bfloat16)
        qc_ref[0, :, 256 * pr + 128:256 * pr + 256] = (rope * C_SCALE).astype(jnp.bfloat16)
        kct_ref[0, 0, 256 * pr:256 * pr + 128, :] = kv[:, 128 * pr:128 * pr + 128].T.astype(jnp.bfloat16)
        kct_ref[0, 0, 256 * pr + 128:256 * pr + 256, :] = kr_t


def _const_spec(shape):
    nd = len(shape)
    return pl.BlockSpec(shape, lambda *_: (0,) * nd, pipeline_mode=pl.Buffered(1))


def _proj_call(x, ctx, mods, gpre, w_ext, qn, wq, kvn, wkv, tables):
    bsz, n_lat, d = x.shape
    n_lat_blocks = n_lat // TQ
    nb = n_lat_blocks + 1
    t = nb * TQ
    tok = lambda w: pl.BlockSpec((1, TQ, w), lambda b, i: (b, i, 0))
    tbl = pl.BlockSpec((TQ, LANES), lambda b, i: (i, 0))
    kt = lambda r: pl.BlockSpec((1, 1, r, TQ), lambda b, i: (b, i, 0, 0))
    x_spec = pl.BlockSpec((1, TQ, d), lambda b, i: (b, jnp.minimum(i, n_lat_blocks - 1), 0))
    ctx_spec = pl.BlockSpec((1, TQ, d), lambda b, i: (b, 0, 0))
    mod_spec = pl.BlockSpec((1, 6, d), lambda b, i: (2 * b + (i >= n_lat_blocks).astype(jnp.int32), 0, 0))
    bf = jnp.bfloat16
    out_shape = [jax.ShapeDtypeStruct((bsz, t, 384), bf), jax.ShapeDtypeStruct((bsz, t, 256), bf),
                 jax.ShapeDtypeStruct((bsz, t, 256), bf), jax.ShapeDtypeStruct((bsz, t, 256), bf),
                 jax.ShapeDtypeStruct((bsz, nb, 256, TQ), bf), jax.ShapeDtypeStruct((bsz, t, 256), bf),
                 jax.ShapeDtypeStruct((bsz, t, 768), bf), jax.ShapeDtypeStruct((bsz, nb, 768, TQ), bf),
                 jax.ShapeDtypeStruct((bsz, t, 384), bf)]
    out_specs = [tok(384), tok(256), tok(256), tok(256), kt(256), tok(256), tok(768), kt(768), tok(384)]
    return pl.pallas_call(
        functools.partial(_proj_kernel, n_lat_blocks=n_lat_blocks),
        grid=(bsz, nb),
        in_specs=[x_spec, ctx_spec, mod_spec, _const_spec((1, d)), _const_spec(w_ext.shape),
                  _const_spec(qn.shape), _const_spec(wq.shape), _const_spec(kvn.shape), _const_spec(wkv.shape),
                  tbl, tbl, tbl, tbl],
        out_specs=out_specs,
        out_shape=out_shape,
        compiler_params=pltpu.CompilerParams(dimension_semantics=("arbitrary", "arbitrary"),
                                             vmem_limit_bytes=VMEM_LIMIT),
        name="proj",
    )(x, ctx, mods, gpre, w_ext, qn, wq, kvn, wkv, *tables)


def _attn_a_kernel(sink_ref, q_ref, k_ref, v_ref, o_ref, bias_ref, *, n_lat):
    i = pl.program_id(1)
    is_ctx = i == n_lat // TQ
    t0 = i * TQ
    start = pl.multiple_of(jnp.clip(t0 - WINDOW, 0, n_lat - BAND), WINDOW)
    kcat = jnp.concatenate([k_ref[0, pl.ds(start, BAND), :], k_ref[0, n_lat:, :]], axis=0)
    vcat = jnp.concatenate([v_ref[0, pl.ds(start, BAND), :], v_ref[0, n_lat:, :]], axis=0)
    qpos = t0 + jnp.where(is_ctx, 4 * n_lat, 0) + lax.broadcasted_iota(jnp.int32, (TQ, BAND), 0)
    kpos = start + lax.broadcasted_iota(jnp.int32, (TQ, BAND), 1)
    bias_ref[...] = jnp.where(jnp.abs(qpos - kpos) <= WINDOW, 0.0, NEG_INF)
    low_half = lax.broadcasted_iota(jnp.int32, (TQ, LANES), 1) < HEAD_DIM

    def scores(head):
        qg = q_ref[0, :, LANES * (head // 2):LANES * (head // 2 + 1)]
        kv = head // A_GROUP
        qm = jnp.where(low_half if head % 2 == 0 else jnp.logical_not(low_half), qg, jnp.zeros_like(qg))
        return _dot_nt(qm, kcat[:, LANES * kv:LANES * (kv + 1)])

    outs = []
    nxt = scores(0)
    for head in range(A_HEADS):
        s = nxt
        if head + 1 < A_HEADS:
            nxt = scores(head + 1)
        kv = head // A_GROUP
        n_parts = s.shape[1] // LANES
        parts = [s[:, k * LANES:(k + 1) * LANES] for k in range(n_parts)]
        parts = [part + bias_ref[:, k * LANES:(k + 1) * LANES] if k < BAND // LANES else part
                 for k, part in enumerate(parts)]
        sink = sink_ref[head] * LOG2E
        m = jnp.maximum(jnp.max(functools.reduce(jnp.maximum, parts), axis=1, keepdims=True), sink)
        ps = [jnp.exp2(part - m) for part in parts]
        l = jnp.sum(functools.reduce(jnp.add, ps), axis=1, keepdims=True) + jnp.exp2(sink - m)
        pb = jnp.concatenate([p.astype(jnp.bfloat16) for p in ps], axis=1)
        outs.append(_dot(pb, vcat[:, LANES * kv:LANES * (kv + 1)]) / l)
        if head % 2 == 1:
            pr = head // 2
            o_ref[0, :, LANES * pr:LANES * (pr + 1)] = jnp.where(low_half, outs[-2], outs[-1]).astype(o_ref.dtype)


def _attn_a_call(sink, qa, ka, va, n_lat, nqb):
    bsz, t, _ = qa.shape
    return pl.pallas_call(
        functools.partial(_attn_a_kernel, n_lat=n_lat),
        grid=(bsz, nqb),
        in_specs=[pl.BlockSpec(memory_space=pltpu.SMEM),
                  pl.BlockSpec((1, TQ, 384), lambda b, i: (b, i, 0)),
                  pl.BlockSpec((1, t, 256), lambda b, i: (b, 0, 0)),
                  pl.BlockSpec((1, t, 256), lambda b, i: (b, 0, 0))],
        out_specs=pl.BlockSpec((1, TQ, 384), lambda b, i: (b, i, 0)),
        out_shape=jax.ShapeDtypeStruct((bsz, nqb * TQ, 384), jnp.bfloat16),
        scratch_shapes=[pltpu.VMEM((TQ, BAND), jnp.float32)],
        compiler_params=pltpu.CompilerParams(dimension_semantics=("arbitrary", "arbitrary"),
                                             vmem_limit_bytes=VMEM_LIMIT),
        name="attn_a",
    )(sink, qa, ka, va)


def _flash_init(m_ref, l_ref, acc_ref):
    m_ref[...] = jnp.full(m_ref.shape, NEG_INF, jnp.float32)
    l_ref[...] = jnp.zeros(l_ref.shape, jnp.float32)
    acc_ref[...] = jnp.zeros(acc_ref.shape, jnp.float32)


def _flash_scores(qm, kts):
    parts = []
    for kt in kts:
        s = _dot(qm, kt)
        parts += [s[:, k * LANES:(k + 1) * LANES] for k in range(TK // LANES)]
    return parts


def _flash_update(j, m_ref, l_ref, acc_ref, parts, v):
    m_old = m_ref[j]
    m_new = jnp.maximum(m_old, jnp.max(functools.reduce(jnp.maximum, parts), axis=1, keepdims=True))
    alpha = jnp.exp2(m_old - m_new)
    ps = [jnp.exp2(part - m_new) for part in parts]
    l_ref[j] = alpha * l_ref[j] + functools.reduce(jnp.add, ps)
    pb = jnp.concatenate([p.astype(jnp.bfloat16) for p in ps], axis=1)
    acc_ref[j] = alpha * acc_ref[j] + _dot(pb, v)
    m_ref[j] = m_new


def _key_steps(n_tiles, size):
    return [(c, min(size, n_tiles - c)) for c in range(0, n_tiles, size)]


def _flash_run(m_ref, l_ref, acc_ref, qms, steps, kts_of, v_of):
    seq = [(c0, n, j) for (c0, n) in steps for j in range(len(qms))]
    scores = lambda item: _flash_scores(qms[item[2]], kts_of(*item))
    pending = [scores(item) for item in seq[:LOOKAHEAD]]
    for idx, (c0, n, j) in enumerate(seq):
        if idx + LOOKAHEAD < len(seq):
            pending.append(scores(seq[idx + LOOKAHEAD]))
        _flash_update(j, m_ref, l_ref, acc_ref, pending.pop(0), v_of(c0, n, j))


def _single_pass_run(acc_ref, qms, groups, kts_of, v_of):
    seq = [(j, gi) for j in range(len(qms)) for gi in range(len(groups))]
    scores = lambda item: _flash_scores(qms[item[0]], kts_of(*groups[item[1]], item[0]))
    pending = [scores(seq[0])]
    l_max = None
    for idx, (j, gi) in enumerate(seq):
        if idx + 1 < len(seq):
            pending.append(scores(seq[idx + 1]))
        parts = pending.pop(0)
        if gi == 0:
            shift = jnp.max(functools.reduce(jnp.maximum, parts[:TK // LANES]), axis=1, keepdims=True)
            l = acc = None
        ps = [jnp.exp2(part - shift) for part in parts]
        l_g = functools.reduce(jnp.add, ps)
        pb = jnp.concatenate([p.astype(jnp.bfloat16) for p in ps], axis=1)
        pv = _dot(pb, v_of(*groups[gi], j))
        l, acc = (l_g, pv) if gi == 0 else (l + l_g, acc + pv)
        if gi == len(groups) - 1:
            l_row = jnp.sum(l, axis=1, keepdims=True)
            acc_ref[j] = acc / l_row
            l_max = jnp.max(l_row) if l_max is None else jnp.maximum(l_max, jnp.max(l_row))
    return l_max


def _diff_single_pass_run(acc_ref, p_ref, qms, groups, kts_of, v_of, lam):
    n_g = len(groups)
    tasks = []
    for h in range(len(qms) // 2):
        tasks += [("scores", h, m, gi) for m in range(2) for gi in range(n_g)]
        tasks += [("values", h, 0, gi) for gi in range(n_g)]
    score_tasks = [t for t in tasks if t[0] == "scores"]
    pending = []

    def issue():
        if score_tasks:
            _, h, m, gi = score_tasks.pop(0)
            pending.append(_flash_scores(qms[2 * h + m], kts_of(*groups[gi], 2 * h + m)))

    issue()
    l_max = None
    shift, l = [None, None], [None, None]
    for kind, h, m, gi in tasks:
        c0, n = groups[gi]
        keys = slice(c0 * TK, (c0 + n) * TK)
        if kind == "scores":
            parts = pending.pop(0)
            issue()
            if gi == 0:
                shift[m] = jnp.max(functools.reduce(jnp.maximum, parts[:TK // LANES]), axis=1, keepdims=True)
            ps = [jnp.exp2(part - shift[m]) for part in parts]
            l_g = functools.reduce(jnp.add, ps)
            l[m] = l_g if gi == 0 else l[m] + l_g
            p_ref[2 * (h % 2) + m, :, keys] = jnp.concatenate([p.astype(jnp.bfloat16) for p in ps], axis=1)
        else:
            if gi == 0:
                l0 = jnp.sum(l[0], axis=1, keepdims=True)
                l1 = jnp.sum(l[1], axis=1, keepdims=True)
                coef = lam * l0 / l1
                big = jnp.maximum(jnp.max(l0), jnp.max(l1))
                l_max = big if l_max is None else jnp.maximum(l_max, big)
            a = (p_ref[2 * (h % 2), :, keys].astype(jnp.float32)
                 - coef * p_ref[2 * (h % 2) + 1, :, keys].astype(jnp.float32)).astype(jnp.bfloat16)
            pv = _dot(a, v_of(c0, n, 2 * h))
            acc = pv if gi == 0 else acc + pv
            if gi == n_g - 1:
                acc_ref[h] = acc / l0
    return l_max


def _online_softmax(m_ref, l_ref, acc_ref, qms, steps, kts_of, v_of):
    _flash_init(m_ref, l_ref, acc_ref)
    _flash_run(m_ref, l_ref, acc_ref, qms, steps, kts_of, v_of)
    for j in range(len(qms)):
        acc_ref[j] = acc_ref[j] / jnp.sum(l_ref[j], axis=1, keepdims=True)


def _attend(fast, exact, is_ctx, n_lat_tiles):
    @pl.when(is_ctx)
    def _():
        fast([(n_lat_tiles, 1)])

    @pl.when(jnp.logical_not(is_ctx))
    def _():
        l_max = fast(_key_steps(n_lat_tiles + 1, GROUP))

        @pl.when(jnp.logical_not(l_max < ROW_SUM_LIMIT))
        def _():
            exact(_key_steps(n_lat_tiles + 1, SUB))


def _resident_spec(shape):
    nd = len(shape)
    return pl.BlockSpec(shape, lambda b, i: (b,) + (0,) * (nd - 1), pipeline_mode=pl.Buffered(1))


def _flash_scratch(n_maps):
    return [pltpu.VMEM((n_maps, TQ, LANES), jnp.float32) for _ in range(3)]


def _attn_b_kernel(lam_ref, sub_ref, q_ref, kt_ref, v_ref, o_ref, m_ref, l_ref, acc_ref, p_ref, *,
                   n_lat, lam_init):
    i = pl.program_id(1)
    n_lat_tiles = n_lat // TK
    is_ctx = i == n_lat // TQ
    lam4 = lam_ref[...]
    lam = (jnp.exp(jnp.sum(lam4[0:1] * lam4[1:2], axis=-1, keepdims=True))
           - jnp.exp(jnp.sum(lam4[2:3] * lam4[3:4], axis=-1, keepdims=True)) + lam_init)

    lane = lax.broadcasted_iota(jnp.int32, (TQ, LANES), 1)
    n_maps = 2 * B_HEADS
    qms = []
    for j in range(n_maps):
        qg = q_ref[0, :, LANES * (j // 4):LANES * (j // 4 + 1)]
        qms.append(jnp.where((lane // B_QK_DIM) == (j % 4), qg, jnp.zeros_like(qg)))

    def kts_of(c0, n, j):
        g = j // 4
        return [kt_ref[0, c0 + k, LANES * g:LANES * (g + 1), :] for k in range(n)]

    def v_of(c0, n, j):
        g = j // 4
        return v_ref[0, c0 * TK:(c0 + n) * TK, LANES * g:LANES * (g + 1)]

    def exact(steps):
        _online_softmax(m_ref, l_ref, acc_ref, qms, steps, kts_of, v_of)
        for h in range(B_HEADS):
            acc_ref[h] = acc_ref[2 * h] - lam * acc_ref[2 * h + 1]

    _attend(lambda groups: _diff_single_pass_run(acc_ref, p_ref, qms, groups, kts_of, v_of, lam),
            exact, is_ctx, n_lat_tiles)

    low_half = lane < B_V_DIM
    for g in range(2):
        o = jnp.where(low_half, acc_ref[2 * g], acc_ref[2 * g + 1])
        sq = o * o
        ms_lo = jnp.sum(jnp.where(low_half, sq, 0.0), axis=-1, keepdims=True) * (1.0 / B_V_DIM)
        ms_hi = jnp.sum(jnp.where(low_half, 0.0, sq), axis=-1, keepdims=True) * (1.0 / B_V_DIM)
        ms = jnp.where(low_half, ms_lo, ms_hi)
        y = o * lax.rsqrt(ms + NORM_EPS) * sub_ref[...] * (1.0 - lam_init)
        o_ref[0, :, LANES * g:LANES * (g + 1)] = y.astype(o_ref.dtype)


def _attn_b_call(lam4, sub, qb, kbt, vb, n_lat, lam_init, nqb):
    bsz, t, _ = qb.shape
    nb = t // TQ
    return pl.pallas_call(
        functools.partial(_attn_b_kernel, n_lat=n_lat, lam_init=lam_init),
        grid=(bsz, nqb),
        in_specs=[pl.BlockSpec((4, B_QK_DIM), lambda b, i: (0, 0)),
                  pl.BlockSpec((1, LANES), lambda b, i: (0, 0)),
                  pl.BlockSpec((1, TQ, 256), lambda b, i: (b, i, 0)),
                  _resident_spec((1, nb, 256, TK)), _resident_spec((1, t, 256))],
        out_specs=pl.BlockSpec((1, TQ, 256), lambda b, i: (b, i, 0)),
        out_shape=jax.ShapeDtypeStruct((bsz, nqb * TQ, 256), jnp.bfloat16),
        scratch_shapes=_flash_scratch(2 * B_HEADS) + [pltpu.VMEM((4, TQ, t), jnp.bfloat16)],
        compiler_params=pltpu.CompilerParams(dimension_semantics=("arbitrary", "arbitrary"),
                                             vmem_limit_bytes=VMEM_LIMIT),
        name="attn_b",
    )(lam4, sub, qb, kbt, vb)


def _attn_c_kernel(q_ref, kt_ref, v_ref, o_ref, m_ref, l_ref, acc_ref, *, n_lat):
    i = pl.program_id(1)
    n_lat_tiles = n_lat // TK
    is_ctx = i == n_lat // TQ
    lane2 = lax.broadcasted_iota(jnp.int32, (TQ, 2 * LANES), 1)
    even = (lane2 < 64) | ((lane2 >= 128) & (lane2 < 160))
    odd = ((lane2 >= 64) & (lane2 < 128)) | ((lane2 >= 160) & (lane2 < 192))
    qms = []
    for h in range(C_HEADS):
        qp = q_ref[0, :, 256 * (h // 2):256 * (h // 2 + 1)]
        qms.append(jnp.where(even if h % 2 == 0 else odd, qp, jnp.zeros_like(qp)))

    def kts_of(c0, n, h):
        pr = h // 2
        return [kt_ref[0, c0 + k, 256 * pr:256 * (pr + 1), :] for k in range(n)]

    def v_of(c0, n, h):
        pr = h // 2
        return v_ref[0, c0 * TK:(c0 + n) * TK, LANES * pr:LANES * (pr + 1)]

    _attend(lambda groups: _single_pass_run(acc_ref, qms, groups, kts_of, v_of),
            lambda steps: _online_softmax(m_ref, l_ref, acc_ref, qms, steps, kts_of, v_of),
            is_ctx, n_lat_tiles)

    low_half = lax.broadcasted_iota(jnp.int32, (TQ, LANES), 1) < C_V
    for pr in range(C_HEADS // 2):
        o = jnp.where(low_half, acc_ref[2 * pr], acc_ref[2 * pr + 1])
        o_ref[0, :, LANES * pr:LANES * (pr + 1)] = o.astype(o_ref.dtype)


def _attn_c_call(qc, kct, vc, n_lat, nqb):
    bsz, t, _ = qc.shape
    nb = t // TQ
    return pl.pallas_call(
        functools.partial(_attn_c_kernel, n_lat=n_lat),
        grid=(bsz, nqb),
        in_specs=[pl.BlockSpec((1, TQ, 768), lambda b, i: (b, i, 0)),
                  _resident_spec((1, nb, 768, TK)), _resident_spec((1, t, 384))],
        out_specs=pl.BlockSpec((1, TQ, 384), lambda b, i: (b, i, 0)),
        out_shape=jax.ShapeDtypeStruct((bsz, nqb * TQ, 384), jnp.bfloat16),
        scratch_shapes=_flash_scratch(C_HEADS),
        compiler_params=pltpu.CompilerParams(dimension_semantics=("arbitrary", "arbitrary"),
                                             vmem_limit_bytes=VMEM_LIMIT),
        name="attn_c",
    )(qc, kct, vc)


def _post_kernel(x_ref, mod_ref, oa_ref, ob_ref, oc_ref, wout_ref, gpm_ref, gpf_ref, gqf_ref,
                 wg_ref, wu_ref, wd_ref, o_ref):
    mod = mod_ref[0]
    tm = x_ref.shape[1]
    nh = 2 if tm >= 2 * TQ else 1
    hs = [slice(k * (tm // nh), (k + 1) * (tm // nh)) for k in range(nh)]
    mixd = [_dot(jnp.concatenate([oa_ref[0, h], ob_ref[0, h], oc_ref[0, h]], axis=1), wout_ref[...]) for h in hs]
    xs, gs, us = [], [], []
    for k, h in enumerate(hs):
        x = x_ref[0, h] + mod[2:3] * _rms(mixd[k], gpm_ref[...])
        hb = (_rms(x, gpf_ref[...]) * (1.0 + mod[4:5]) + mod[3:4]).astype(jnp.bfloat16)
        xs.append(x)
        gs.append(_dot(hb, wg_ref[...]))
        us.append(_dot(hb, wu_ref[...]))
    ffs = []
    for k in range(nh):
        act = (gs[k] * jax.nn.sigmoid(gs[k]) * us[k]).astype(jnp.bfloat16)
        ffs.append(_dot(act, wd_ref[...]))
    for k, h in enumerate(hs):
        o_ref[0, h] = xs[k] + mod[5:6] * _rms(ffs[k], gqf_ref[...])


def _post_call(x, mods, oa, ob, oc, wout, gpm, gpf, gqf, wg, wu, wd, *, tm, blk0, mod_row):
    bsz, n, d = x.shape
    tok = lambda w: pl.BlockSpec((1, tm, w), lambda i, b: (b, i, 0))
    att = lambda w: pl.BlockSpec((1, tm, w), lambda i, b: (b, blk0 + i, 0))
    mod_spec = pl.BlockSpec((1, 6, d), lambda i, b: (2 * b + mod_row, 0, 0))
    return pl.pallas_call(
        _post_kernel,
        grid=(n // tm, bsz),
        in_specs=[tok(d), mod_spec, att(384), att(256), att(384),
                  _const_spec(wout.shape), _const_spec((1, d)), _const_spec((1, d)), _const_spec((1, d)),
                  _const_spec(wg.shape), _const_spec(wu.shape), _const_spec(wd.shape)],
        out_specs=tok(d),
        out_shape=jax.ShapeDtypeStruct((bsz, n, d), jnp.float32),
        compiler_params=pltpu.CompilerParams(dimension_semantics=("arbitrary", "arbitrary"),
                                             vmem_limit_bytes=VMEM_LIMIT),
        name="post",
    )(x, mods, oa, ob, oc, wout, gpm, gpf, gqf, wg, wu, wd)


def kernel(x, c, ctx, c_ctx, w_ada, b_ada, g_pre_mix, g_post_mix, w_in, win_sink, diff_lambda_q1, diff_lambda_k1, diff_lambda_q2, diff_lambda_k2, diff_sub_norm, mla_q_norm, mla_w_q_up, mla_kv_norm, mla_w_kv_up, w_out, g_pre_ffn, g_post_ffn, w_gate, w_up, w_down):
    bsz, n_lat, d = x.shape
    n_ctx = ctx.shape[1]
    depth = w_ada.shape[0]
    assert d == D_MODEL and n_lat % TM_POST == 0 and n_ctx == TQ and n_lat >= BAND
    n_lat_blocks = n_lat // TQ
    bf = jnp.bfloat16

    cvec = jnp.zeros((8, d), jnp.float32).at[:bsz].set(c).at[bsz].set(c_ctx)
    mod_all = _ada_call(cvec, w_ada, b_ada)
    tables = _rope_tables(n_lat, n_ctx)

    ext_cols = _ext_columns()
    qup_cols = _qup_columns()
    kvup_cols = _kvup_columns()

    for l in range(depth):
        update_ctx = l < depth - 1
        m = mod_all[l].reshape(8, 6, d)
        mods = jnp.stack([m[:bsz], jnp.broadcast_to(m[bsz], (bsz, 6, d))], axis=1).reshape(2 * bsz, 6, d)

        w_ext = jnp.concatenate([w_in[l], jnp.zeros((d, 1), jnp.float32)], axis=1)[:, ext_cols].astype(bf)
        wq = jnp.concatenate([mla_w_q_up[l], jnp.zeros((C_Q_RANK, 1), jnp.float32)], axis=1)[:, qup_cols].astype(bf)
        wkv = mla_w_kv_up[l][:, kvup_cols].astype(bf)

        qa, ka, va, qb, kbt, vb, qc, kct, vc = _proj_call(
            x, ctx, mods, g_pre_mix[l][None], w_ext, mla_q_norm[l][None], wq, mla_kv_norm[l][None], wkv, tables)

        lam_init = 0.8 - 0.6 * math.exp(-0.3 * l)
        lam4 = jnp.stack([diff_lambda_q1[l], diff_lambda_k1[l], diff_lambda_q2[l], diff_lambda_k2[l]])
        sub = jnp.concatenate([diff_sub_norm[l], diff_sub_norm[l]])[None]

        nqb = n_lat_blocks + (1 if update_ctx else 0)
        oa = _attn_a_call(win_sink[l], qa, ka, va, n_lat, nqb)
        ob = _attn_b_call(lam4, sub, qb, kbt, vb, n_lat, lam_init, nqb)
        oc = _attn_c_call(qc, kct, vc, n_lat, nqb)

        weights = (w_out[l].astype(bf), g_post_mix[l][None], g_pre_ffn[l][None], g_post_ffn[l][None],
                   w_gate[l].astype(bf), w_up[l].astype(bf), w_down[l].astype(bf))
        x_new = _post_call(x, mods, oa, ob, oc, *weights, tm=TM_POST, blk0=0, mod_row=0)
        if update_ctx:
            ctx = _post_call(ctx, mods, oa, ob, oc, *weights, tm=TQ, blk0=n_lat_blocks, mod_row=1)
        x = x_new
    return x
```

```python
import functools
import math

import numpy as np
import jax
import jax.numpy as jnp
from jax import lax
from jax.experimental import pallas as pl
from jax.experimental.pallas import tpu as pltpu

D_MODEL = 1024
GRID_W = 64
HEAD_DIM = 64
A_HEADS = 6
A_KV_HEADS = 2
A_GROUP = A_HEADS // A_KV_HEADS
WINDOW = 128
B_HEADS = 4
B_QK_DIM = 32
B_V_DIM = 64
C_HEADS = 6
C_Q_RANK = 256
C_KV_RANK = 128
C_NOPE = 64
C_ROPE = 32
C_V = 64
D_FF = 2816
ROPE_BASE = 10000.0
NORM_EPS = 1e-6
NEG_INF = -1e30
LOG2E = math.log2(math.e)
A_SCALE = HEAD_DIM ** -0.5 * LOG2E
B_SCALE = B_QK_DIM ** -0.5 * LOG2E
C_SCALE = (C_NOPE + C_ROPE) ** -0.5 * LOG2E

LANES = 128
TQ = 256
TM_POST = 512
TK = 256
SUB = 11
LOOKAHEAD = 1
GROUP = 11
ROW_SUM_LIMIT = 2.0 ** 100
BAND = TQ + 2 * WINDOW
VMEM_LIMIT = 56 * 1024 * 1024

_AQ, _AK, _AV = 0, 384, 512
_DQ, _DK, _DV = 640, 896, 1152
_MQ, _MKV, _MKR = 1408, 1664, 1792
_ZERO_COL = 1824

N_ROPED = 384 + 256 + 256 + 256 + 128
N_PLAIN = 256 + 256 + 256 + 128
N_EXT = 2 * N_ROPED + N_PLAIN


def _rope_partner(unit):
    half = unit // 2
    quarter = half // 2
    d = np.arange(unit)
    r = d % half
    partner = np.where(r < quarter, d + quarter, d - quarter)
    sign = np.where(r < quarter, -1.0, 1.0).astype(np.float32)
    return partner, sign


def _ext_columns():
    pa, _ = _rope_partner(HEAD_DIM)
    pb, _ = _rope_partner(B_QK_DIM)

    def rot(src, unit, partner):
        return (src // unit) * unit + partner[src % unit]

    qa = np.arange(A_HEADS * HEAD_DIM)
    dup = np.concatenate([np.arange(64), np.arange(64), 64 + np.arange(64), 64 + np.arange(64)])
    qb = np.arange(256)
    kr = np.concatenate([np.arange(32), np.arange(32)])
    zeros64 = np.full((64,), _ZERO_COL)
    main = [_AQ + qa, _AK + dup, _DQ + qb, _DK + qb, np.concatenate([_MKR + kr, zeros64])]
    rots = [_AQ + rot(qa, 64, pa), _AK + rot(dup, 64, pa), _DQ + rot(qb, 32, pb), _DK + rot(qb, 32, pb),
            np.concatenate([_MKR + rot(kr, 32, pb), zeros64])]
    plain = [_AV + dup, _DV + qb, _MQ + np.arange(256), _MKV + np.arange(128)]
    cols = np.concatenate(main + rots + plain)
    assert cols.shape[0] == N_EXT
    return cols


def _qup_columns():
    pb, _ = _rope_partner(C_ROPE)
    width = C_NOPE + C_ROPE
    zero = C_HEADS * width
    zeros64 = np.full((64,), zero)
    main, rots = [], []
    for p in range(C_HEADS // 2):
        h0, h1 = 2 * p, 2 * p + 1
        main += [h0 * width + np.arange(64), h1 * width + np.arange(64),
                 h0 * width + 64 + np.arange(32), h1 * width + 64 + np.arange(32), zeros64]
        rots += [h0 * width + 64 + pb, h1 * width + 64 + pb, zeros64]
    return np.concatenate(main + rots)


def _kvup_columns():
    width = C_NOPE + C_V
    k = np.concatenate([h * width + np.arange(64) for h in range(C_HEADS)])
    v = np.concatenate([h * width + 64 + np.arange(64) for h in range(C_HEADS)])
    return np.concatenate([k, v])


def _rope_tables(n_lat, n_ctx):
    t = jnp.arange(n_lat)
    row = (t // GRID_W).astype(jnp.float32)
    col = (t % GRID_W).astype(jnp.float32)

    def table(unit):
        half = unit // 2
        quarter = half // 2
        _, sign = _rope_partner(unit)
        lane = np.arange(LANES)
        d = lane % unit
        axis = d // half
        j = d % quarter
        freqs = ROPE_BASE ** (-jnp.arange(0, half, 2, dtype=jnp.float32) / half)
        f = freqs[j]
        pos = jnp.where(jnp.asarray(axis)[None, :] == 0, row[:, None], col[:, None])
        ang = pos * f[None, :]
        cos = jnp.cos(ang)
        sin = jnp.sin(ang) * jnp.asarray(sign[d])[None, :]
        cos = jnp.concatenate([cos, jnp.ones((n_ctx, LANES), jnp.float32)], axis=0)
        sin = jnp.concatenate([sin, jnp.zeros((n_ctx, LANES), jnp.float32)], axis=0)
        return cos, sin

    cos_a, sin_a = table(HEAD_DIM)
    cos_b, sin_b = table(B_QK_DIM)
    return cos_a, sin_a, cos_b, sin_b


def _rms(x, g):
    return x * lax.rsqrt(jnp.mean(x * x, axis=-1, keepdims=True) + NORM_EPS) * g


def _dot(a, b):
    return jnp.dot(a, b, preferred_element_type=jnp.float32)


def _dot_nt(a, b):
    return lax.dot_general(a, b, (((1,), (1,)), ((), ())), preferred_element_type=jnp.float32)


def _ada_kernel(c_ref, w_ref, b_ref, o_ref):
    c = c_ref[...]
    h = c * jax.nn.sigmoid(c)
    o_ref[0] = _dot(h, w_ref[0]) + b_ref[0]


def _ada_call(cvec, w_ada, b_ada):
    depth, d, n = w_ada.shape
    tn = 1536
    return pl.pallas_call(
        _ada_kernel,
        grid=(depth, n // tn),
        in_specs=[pl.BlockSpec((8, d), lambda l, j: (0, 0)),
                  pl.BlockSpec((1, d, tn), lambda l, j: (l, 0, j)),
                  pl.BlockSpec((1, 1, tn), lambda l, j: (l, 0, j))],
        out_specs=pl.BlockSpec((1, 8, tn), lambda l, j: (l, 0, j)),
        out_shape=jax.ShapeDtypeStruct((depth, 8, n), jnp.float32),
        compiler_params=pltpu.CompilerParams(dimension_semantics=("arbitrary", "arbitrary"),
                                             vmem_limit_bytes=VMEM_LIMIT),
        name="adaln",
    )(cvec, w_ada, b_ada.reshape(depth, 1, n))


def _proj_kernel(x_ref, ctx_ref, mod_ref, gpre_ref, w_ref, qn_ref, wq_ref, kvn_ref, wkv_ref,
                 ca_ref, sa_ref, cb_ref, sb_ref,
                 qa_ref, ka_ref, va_ref, qb_ref, kbt_ref, vb_ref, qc_ref, kct_ref, vc_ref, *, n_lat_blocks):
    mod = mod_ref[0]
    x = jnp.where(pl.program_id(1) == n_lat_blocks, ctx_ref[0], x_ref[0])
    h = _rms(x, gpre_ref[...]) * (1.0 + mod[1:2]) + mod[0:1]
    p = _dot(h.astype(jnp.bfloat16), w_ref[...])

    ca, sa, cb, sb = ca_ref[...], sa_ref[...], cb_ref[...], sb_ref[...]

    def roped(off, width, cos, sin):
        n = width // LANES
        main = p[:, off:off + width]
        rot = p[:, N_ROPED + off:N_ROPED + off + width]
        return main * jnp.concatenate([cos] * n, axis=1) + rot * jnp.concatenate([sin] * n, axis=1)

    qa_ref[0] = (roped(0, 384, ca, sa) * A_SCALE).astype(jnp.bfloat16)
    ka_ref[0] = roped(384, 256, ca, sa).astype(jnp.bfloat16)
    qb_ref[0] = (roped(640, 256, cb, sb) * B_SCALE).astype(jnp.bfloat16)
    kbt_ref[0, 0] = roped(896, 256, cb, sb).T.astype(jnp.bfloat16)
    kr = roped(1152, 128, cb, sb)

    o = 2 * N_ROPED
    va_ref[0] = p[:, o:o + 256].astype(jnp.bfloat16)
    vb_ref[0] = p[:, o + 256:o + 512].astype(jnp.bfloat16)
    cq = p[:, o + 512:o + 768]
    ckv = p[:, o + 768:o + 896]

    qc = _dot(_rms(cq, qn_ref[...]).astype(jnp.bfloat16), wq_ref[...])
    kv = _dot(_rms(ckv, kvn_ref[...]).astype(jnp.bfloat16), wkv_ref[...])
    vc_ref[0] = kv[:, 384:].astype(jnp.bfloat16)
    kr_t = kr.T.astype(jnp.bfloat16)
    for pr in range(C_HEADS // 2):
        nope = qc[:, 256 * pr:256 * pr + 128]
        rope = (qc[:, 256 * pr + 128:256 * pr + 256] * cb
                + qc[:, 768 + 128 * pr:768 + 128 * pr + 128] * sb)
        qc_ref[0, :, 256 * pr:256 * pr + 128] = (nope * C_SCALE).astype(jnp.bfloat16)
        qc_ref[0, :, 256 * pr + 128:256 * pr + 256] = (rope * C_SCALE).astype(jnp.bfloat16)
        kct_ref[0, 0, 256 * pr:256 * pr + 128, :] = kv[:, 128 * pr:128 * pr + 128].T.astype(jnp.bfloat16)
        kct_ref[0, 0, 256 * pr + 128:256 * pr + 256, :] = kr_t


def _const_spec(shape):
    nd = len(shape)
    return pl.BlockSpec(shape, lambda *_: (0,) * nd, pipeline_mode=pl.Buffered(1))


def _proj_call(x, ctx, mods, gpre, w_ext, qn, wq, kvn, wkv, tables):
    bsz, n_lat, d = x.shape
    n_lat_blocks = n_lat // TQ
    nb = n_lat_blocks + 1
    t = nb * TQ
    tok = lambda w: pl.BlockSpec((1, TQ, w), lambda b, i: (b, i, 0))
    tbl = pl.BlockSpec((TQ, LANES), lambda b, i: (i, 0))
    kt = lambda r: pl.BlockSpec((1, 1, r, TQ), lambda b, i: (b, i, 0, 0))
    x_spec = pl.BlockSpec((1, TQ, d), lambda b, i: (b, jnp.minimum(i, n_lat_blocks - 1), 0))
    ctx_spec = pl.BlockSpec((1, TQ, d), lambda b, i: (b, 0, 0))
    mod_spec = pl.BlockSpec((1, 6, d), lambda b, i: (2 * b + (i >= n_lat_blocks).astype(jnp.int32), 0, 0))
    bf = jnp.bfloat16
    out_shape = [jax.ShapeDtypeStruct((bsz, t, 384), bf), jax.ShapeDtypeStruct((bsz, t, 256), bf),
                 jax.ShapeDtypeStruct((bsz, t, 256), bf), jax.ShapeDtypeStruct((bsz, t, 256), bf),
                 jax.ShapeDtypeStruct((bsz, nb, 256, TQ), bf), jax.ShapeDtypeStruct((bsz, t, 256), bf),
                 jax.ShapeDtypeStruct((bsz, t, 768), bf), jax.ShapeDtypeStruct((bsz, nb, 768, TQ), bf),
                 jax.ShapeDtypeStruct((bsz, t, 384), bf)]
    out_specs = [tok(384), tok(256), tok(256), tok(256), kt(256), tok(256), tok(768), kt(768), tok(384)]
    return pl.pallas_call(
        functools.partial(_proj_kernel, n_lat_blocks=n_lat_blocks),
        grid=(bsz, nb),
        in_specs=[x_spec, ctx_spec, mod_spec, _const_spec((1, d)), _const_spec(w_ext.shape),
                  _const_spec(qn.shape), _const_spec(wq.shape), _const_spec(kvn.shape), _const_spec(wkv.shape),
                  tbl, tbl, tbl, tbl],
        out_specs=out_specs,
        out_shape=out_shape,
        compiler_params=pltpu.CompilerParams(dimension_semantics=("arbitrary", "arbitrary"),
                                             vmem_limit_bytes=VMEM_LIMIT),
        name="proj",
    )(x, ctx, mods, gpre, w_ext, qn, wq, kvn, wkv, *tables)


def _attn_a_kernel(sink_ref, q_ref, k_ref, v_ref, o_ref, bias_ref, *, n_lat):
    i = pl.program_id(1)
    is_ctx = i == n_lat // TQ
    t0 = i * TQ
    start = pl.multiple_of(jnp.clip(t0 - WINDOW, 0, n_lat - BAND), WINDOW)
    kcat = jnp.concatenate([k_ref[0, pl.ds(start, BAND), :], k_ref[0, n_lat:, :]], axis=0)
    vcat = jnp.concatenate([v_ref[0, pl.ds(start, BAND), :], v_ref[0, n_lat:, :]], axis=0)
    qpos = t0 + jnp.where(is_ctx, 4 * n_lat, 0) + lax.broadcasted_iota(jnp.int32, (TQ, BAND), 0)
    kpos = start + lax.broadcasted_iota(jnp.int32, (TQ, BAND), 1)
    bias_ref[...] = jnp.where(jnp.abs(qpos - kpos) <= WINDOW, 0.0, NEG_INF)
    low_half = lax.broadcasted_iota(jnp.int32, (TQ, LANES), 1) < HEAD_DIM

    def scores(head):
        qg = q_ref[0, :, LANES * (head // 2):LANES * (head // 2 + 1)]
        kv = head // A_GROUP
        qm = jnp.where(low_half if head % 2 == 0 else jnp.logical_not(low_half), qg, jnp.zeros_like(qg))
        return _dot_nt(qm, kcat[:, LANES * kv:LANES * (kv + 1)])

    outs = []
    nxt = scores(0)
    for head in range(A_HEADS):
        s = nxt
        if head + 1 < A_HEADS:
            nxt = scores(head + 1)
        kv = head // A_GROUP
        n_parts = s.shape[1] // LANES
        parts = [s[:, k * LANES:(k + 1) * LANES] for k in range(n_parts)]
        parts = [part + bias_ref[:, k * LANES:(k + 1) * LANES] if k < BAND // LANES else part
                 for k, part in enumerate(parts)]
        sink = sink_ref[head] * LOG2E
        m = jnp.maximum(jnp.max(functools.reduce(jnp.maximum, parts), axis=1, keepdims=True), sink)
        ps = [jnp.exp2(part - m) for part in parts]
        l = jnp.sum(functools.reduce(jnp.add, ps), axis=1, keepdims=True) + jnp.exp2(sink - m)
        pb = jnp.concatenate([p.astype(jnp.bfloat16) for p in ps], axis=1)
        outs.append(_dot(pb, vcat[:, LANES * kv:LANES * (kv + 1)]) / l)
        if head % 2 == 1:
            pr = head // 2
            o_ref[0, :, LANES * pr:LANES * (pr + 1)] = jnp.where(low_half, outs[-2], outs[-1]).astype(o_ref.dtype)


def _attn_a_call(sink, qa, ka, va, n_lat, nqb):
    bsz, t, _ = qa.shape
    return pl.pallas_call(
        functools.partial(_attn_a_kernel, n_lat=n_lat),
        grid=(bsz, nqb),
        in_specs=[pl.BlockSpec(memory_space=pltpu.SMEM),
                  pl.BlockSpec((1, TQ, 384), lambda b, i: (b, i, 0)),
                  pl.BlockSpec((1, t, 256), lambda b, i: (b, 0, 0)),
                  pl.BlockSpec((1, t, 256), lambda b, i: (b, 0, 0))],
        out_specs=pl.BlockSpec((1, TQ, 384), lambda b, i: (b, i, 0)),
        out_shape=jax.ShapeDtypeStruct((bsz, nqb * TQ, 384), jnp.bfloat16),
        scratch_shapes=[pltpu.VMEM((TQ, BAND), jnp.float32)],
        compiler_params=pltpu.CompilerParams(dimension_semantics=("arbitrary", "arbitrary"),
                                             vmem_limit_bytes=VMEM_LIMIT),
        name="attn_a",
    )(sink, qa, ka, va)


def _flash_init(m_ref, l_ref, acc_ref):
    m_ref[...] = jnp.full(m_ref.shape, NEG_INF, jnp.float32)
    l_ref[...] = jnp.zeros(l_ref.shape, jnp.float32)
    acc_ref[...] = jnp.zeros(acc_ref.shape, jnp.float32)


def _flash_scores(qm, kts):
    parts = []
    for kt in kts:
        s = _dot(qm, kt)
        parts += [s[:, k * LANES:(k + 1) * LANES] for k in range(TK // LANES)]
    return parts


def _flash_update(j, m_ref, l_ref, acc_ref, parts, v):
    m_old = m_ref[j]
    m_new = jnp.maximum(m_old, jnp.max(functools.reduce(jnp.maximum, parts), axis=1, keepdims=True))
    alpha = jnp.exp2(m_old - m_new)
    ps = [jnp.exp2(part - m_new) for part in parts]
    l_ref[j] = alpha * l_ref[j] + functools.reduce(jnp.add, ps)
    pb = jnp.concatenate([p.astype(jnp.bfloat16) for p in ps], axis=1)
    acc_ref[j] = alpha * acc_ref[j] + _dot(pb, v)
    m_ref[j] = m_new


def _key_steps(n_tiles, size):
    return [(c, min(size, n_tiles - c)) for c in range(0, n_tiles, size)]


def _flash_run(m_ref, l_ref, acc_ref, qms, steps, kts_of, v_of):
    seq = [(c0, n, j) for (c0, n) in steps for j in range(len(qms))]
    scores = lambda item: _flash_scores(qms[item[2]], kts_of(*item))
    pending = [scores(item) for item in seq[:LOOKAHEAD]]
    for idx, (c0, n, j) in enumerate(seq):
        if idx + LOOKAHEAD < len(seq):
            pending.append(scores(seq[idx + LOOKAHEAD]))
        _flash_update(j, m_ref, l_ref, acc_ref, pending.pop(0), v_of(c0, n, j))


def _single_pass_run(acc_ref, qms, groups, kts_of, v_of):
    seq = [(j, gi) for j in range(len(qms)) for gi in range(len(groups))]
    scores = lambda item: _flash_scores(qms[item[0]], kts_of(*groups[item[1]], item[0]))
    pending = [scores(seq[0])]
    l_max = None
    for idx, (j, gi) in enumerate(seq):
        if idx + 1 < len(seq):
            pending.append(scores(seq[idx + 1]))
        parts = pending.pop(0)
        if gi == 0:
            shift = jnp.max(functools.reduce(jnp.maximum, parts[:TK // LANES]), axis=1, keepdims=True)
            l = acc = None
        ps = [jnp.exp2(part - shift) for part in parts]
        l_g = functools.reduce(jnp.add, ps)
        pb = jnp.concatenate([p.astype(jnp.bfloat16) for p in ps], axis=1)
        pv = _dot(pb, v_of(*groups[gi], j))
        l, acc = (l_g, pv) if gi == 0 else (l + l_g, acc + pv)
        if gi == len(groups) - 1:
            l_row = jnp.sum(l, axis=1, keepdims=True)
            acc_ref[j] = acc / l_row
            l_max = jnp.max(l_row) if l_max is None else jnp.maximum(l_max, jnp.max(l_row))
    return l_max


def _diff_single_pass_run(acc_ref, p_ref, qms, groups, kts_of, v_of, lam):
    n_g = len(groups)
    tasks = []
    for h in range(len(qms) // 2):
        tasks += [("scores", h, m, gi) for m in range(2) for gi in range(n_g)]
        tasks += [("values", h, 0, gi) for gi in range(n_g)]
    score_tasks = [t for t in tasks if t[0] == "scores"]
    pending = []

    def issue():
        if score_tasks:
            _, h, m, gi = score_tasks.pop(0)
            pending.append(_flash_scores(qms[2 * h + m], kts_of(*groups[gi], 2 * h + m)))

    issue()
    l_max = None
    shift, l = [None, None], [None, None]
    for kind, h, m, gi in tasks:
        c0, n = groups[gi]
        keys = slice(c0 * TK, (c0 + n) * TK)
        if kind == "scores":
            parts = pending.pop(0)
            issue()
            if gi == 0:
                shift[m] = jnp.max(functools.reduce(jnp.maximum, parts[:TK // LANES]), axis=1, keepdims=True)
            ps = [jnp.exp2(part - shift[m]) for part in parts]
            l_g = functools.reduce(jnp.add, ps)
            l[m] = l_g if gi == 0 else l[m] + l_g
            p_ref[2 * (h % 2) + m, :, keys] = jnp.concatenate([p.astype(jnp.bfloat16) for p in ps], axis=1)
        else:
            if gi == 0:
                l0 = jnp.sum(l[0], axis=1, keepdims=True)
                l1 = jnp.sum(l[1], axis=1, keepdims=True)
                coef = jnp.broadcast_to(lam * l0 / l1, (TQ, LANES)).astype(jnp.bfloat16)
                big = jnp.maximum(jnp.max(l0), jnp.max(l1))
                l_max = big if l_max is None else jnp.maximum(l_max, big)
            a = (p_ref[2 * (h % 2), :, keys]
                 - jnp.concatenate([coef] * (n * TK // LANES), axis=1) * p_ref[2 * (h % 2) + 1, :, keys])
            pv = _dot(a, v_of(c0, n, 2 * h))
            acc = pv if gi == 0 else acc + pv
            if gi == n_g - 1:
                acc_ref[h] = acc / l0
    return l_max


def _online_softmax(m_ref, l_ref, acc_ref, qms, steps, kts_of, v_of):
    _flash_init(m_ref, l_ref, acc_ref)
    _flash_run(m_ref, l_ref, acc_ref, qms, steps, kts_of, v_of)
    for j in range(len(qms)):
        acc_ref[j] = acc_ref[j] / jnp.sum(l_ref[j], axis=1, keepdims=True)


def _attend(fast, exact, is_ctx, n_lat_tiles):
    @pl.when(is_ctx)
    def _():
        fast([(n_lat_tiles, 1)])

    @pl.when(jnp.logical_not(is_ctx))
    def _():
        l_max = fast(_key_steps(n_lat_tiles + 1, GROUP))

        @pl.when(jnp.logical_not(l_max < ROW_SUM_LIMIT))
        def _():
            exact(_key_steps(n_lat_tiles + 1, SUB))


def _resident_spec(shape):
    nd = len(shape)
    return pl.BlockSpec(shape, lambda b, i: (b,) + (0,) * (nd - 1), pipeline_mode=pl.Buffered(1))


def _flash_scratch(n_maps):
    return [pltpu.VMEM((n_maps, TQ, LANES), jnp.float32) for _ in range(3)]


def _attn_b_kernel(lam_ref, sub_ref, q_ref, kt_ref, v_ref, o_ref, m_ref, l_ref, acc_ref, p_ref, *,
                   n_lat, lam_init):
    i = pl.program_id(1)
    n_lat_tiles = n_lat // TK
    is_ctx = i == n_lat // TQ
    lam4 = lam_ref[...]
    lam = (jnp.exp(jnp.sum(lam4[0:1] * lam4[1:2], axis=-1, keepdims=True))
           - jnp.exp(jnp.sum(lam4[2:3] * lam4[3:4], axis=-1, keepdims=True)) + lam_init)

    lane = lax.broadcasted_iota(jnp.int32, (TQ, LANES), 1)
    n_maps = 2 * B_HEADS
    qms = []
    for j in range(n_maps):
        qg = q_ref[0, :, LANES * (j // 4):LANES * (j // 4 + 1)]
        qms.append(jnp.where((lane // B_QK_DIM) == (j % 4), qg, jnp.zeros_like(qg)))

    def kts_of(c0, n, j):
        g = j // 4
        return [kt_ref[0, c0 + k, LANES * g:LANES * (g + 1), :] for k in range(n)]

    def v_of(c0, n, j):
        g = j // 4
        return v_ref[0, c0 * TK:(c0 + n) * TK, LANES * g:LANES * (g + 1)]

    def exact(steps):
        _online_softmax(m_ref, l_ref, acc_ref, qms, steps, kts_of, v_of)
        for h in range(B_HEADS):
            acc_ref[h] = acc_ref[2 * h] - lam * acc_ref[2 * h + 1]

    _attend(lambda groups: _diff_single_pass_run(acc_ref, p_ref, qms, groups, kts_of, v_of, lam),
            exact, is_ctx, n_lat_tiles)

    low_half = lane < B_V_DIM
    for g in range(2):
        o = jnp.where(low_half, acc_ref[2 * g], acc_ref[2 * g + 1])
        sq = o * o
        ms_lo = jnp.sum(jnp.where(low_half, sq, 0.0), axis=-1, keepdims=True) * (1.0 / B_V_DIM)
        ms_hi = jnp.sum(jnp.where(low_half, 0.0, sq), axis=-1, keepdims=True) * (1.0 / B_V_DIM)
        ms = jnp.where(low_half, ms_lo, ms_hi)
        y = o * lax.rsqrt(ms + NORM_EPS) * sub_ref[...] * (1.0 - lam_init)
        o_ref[0, :, LANES * g:LANES * (g + 1)] = y.astype(o_ref.dtype)


def _attn_b_call(lam4, sub, qb, kbt, vb, n_lat, lam_init, nqb):
    bsz, t, _ = qb.shape
    nb = t // TQ
    return pl.pallas_call(
        functools.partial(_attn_b_kernel, n_lat=n_lat, lam_init=lam_init),
        grid=(bsz, nqb),
        in_specs=[pl.BlockSpec((4, B_QK_DIM), lambda b, i: (0, 0)),
                  pl.BlockSpec((1, LANES), lambda b, i: (0, 0)),
                  pl.BlockSpec((1, TQ, 256), lambda b, i: (b, i, 0)),
                  _resident_spec((1, nb, 256, TK)), _resident_spec((1, t, 256))],
        out_specs=pl.BlockSpec((1, TQ, 256), lambda b, i: (b, i, 0)),
        out_shape=jax.ShapeDtypeStruct((bsz, nqb * TQ, 256), jnp.bfloat16),
        scratch_shapes=_flash_scratch(2 * B_HEADS) + [pltpu.VMEM((4, TQ, t), jnp.bfloat16)],
        compiler_params=pltpu.CompilerParams(dimension_semantics=("arbitrary", "arbitrary"),
                                             vmem_limit_bytes=VMEM_LIMIT),
        name="attn_b",
    )(lam4, sub, qb, kbt, vb)


def _attn_c_kernel(q_ref, kt_ref, v_ref, o_ref, m_ref, l_ref, acc_ref, *, n_lat):
    i = pl.program_id(1)
    n_lat_tiles = n_lat // TK
    is_ctx = i == n_lat // TQ
    lane2 = lax.broadcasted_iota(jnp.int32, (TQ, 2 * LANES), 1)
    even = (lane2 < 64) | ((lane2 >= 128) & (lane2 < 160))
    odd = ((lane2 >= 64) & (lane2 < 128)) | ((lane2 >= 160) & (lane2 < 192))
    qms = []
    for h in range(C_HEADS):
        qp = q_ref[0, :, 256 * (h // 2):256 * (h // 2 + 1)]
        qms.append(jnp.where(even if h % 2 == 0 else odd, qp, jnp.zeros_like(qp)))

    def kts_of(c0, n, h):
        pr = h // 2
        return [kt_ref[0, c0 + k, 256 * pr:256 * (pr + 1), :] for k in range(n)]

    def v_of(c0, n, h):
        pr = h // 2
        return v_ref[0, c0 * TK:(c0 + n) * TK, LANES * pr:LANES * (pr + 1)]

    _attend(lambda groups: _single_pass_run(acc_ref, qms, groups, kts_of, v_of),
            lambda steps: _online_softmax(m_ref, l_ref, acc_ref, qms, steps, kts_of, v_of),
            is_ctx, n_lat_tiles)

    low_half = lax.broadcasted_iota(jnp.int32, (TQ, LANES), 1) < C_V
    for pr in range(C_HEADS // 2):
        o = jnp.where(low_half, acc_ref[2 * pr], acc_ref[2 * pr + 1])
        o_ref[0, :, LANES * pr:LANES * (pr + 1)] = o.astype(o_ref.dtype)


def _attn_c_call(qc, kct, vc, n_lat, nqb):
    bsz, t, _ = qc.shape
    nb = t // TQ
    return pl.pallas_call(
        functools.partial(_attn_c_kernel, n_lat=n_lat),
        grid=(bsz, nqb),
        in_specs=[pl.BlockSpec((1, TQ, 768), lambda b, i: (b, i, 0)),
                  _resident_spec((1, nb, 768, TK)), _resident_spec((1, t, 384))],
        out_specs=pl.BlockSpec((1, TQ, 384), lambda b, i: (b, i, 0)),
        out_shape=jax.ShapeDtypeStruct((bsz, nqb * TQ, 384), jnp.bfloat16),
        scratch_shapes=_flash_scratch(C_HEADS),
        compiler_params=pltpu.CompilerParams(dimension_semantics=("arbitrary", "arbitrary"),
                                             vmem_limit_bytes=VMEM_LIMIT),
        name="attn_c",
    )(qc, kct, vc)


def _post_kernel(x_ref, mod_ref, oa_ref, ob_ref, oc_ref, wout_ref, gpm_ref, gpf_ref, gqf_ref,
                 wg_ref, wu_ref, wd_ref, o_ref):
    mod = mod_ref[0]
    tm = x_ref.shape[1]
    nh = 2 if tm >= 2 * TQ else 1
    hs = [slice(k * (tm // nh), (k + 1) * (tm // nh)) for k in range(nh)]
    mixd = [_dot(jnp.concatenate([oa_ref[0, h], ob_ref[0, h], oc_ref[0, h]], axis=1), wout_ref[...]) for h in hs]
    xs, gs, us = [], [], []
    for k, h in enumerate(hs):
        x = x_ref[0, h] + mod[2:3] * _rms(mixd[k], gpm_ref[...])
        hb = (_rms(x, gpf_ref[...]) * (1.0 + mod[4:5]) + mod[3:4]).astype(jnp.bfloat16)
        xs.append(x)
        gs.append(_dot(hb, wg_ref[...]))
        us.append(_dot(hb, wu_ref[...]))
    ffs = []
    for k in range(nh):
        act = (gs[k] * jax.nn.sigmoid(gs[k]) * us[k]).astype(jnp.bfloat16)
        ffs.append(_dot(act, wd_ref[...]))
    for k, h in enumerate(hs):
        o_ref[0, h] = xs[k] + mod[5:6] * _rms(ffs[k], gqf_ref[...])


def _post_call(x, mods, oa, ob, oc, wout, gpm, gpf, gqf, wg, wu, wd, *, tm, blk0, mod_row):
    bsz, n, d = x.shape
    tok = lambda w: pl.BlockSpec((1, tm, w), lambda i, b: (b, i, 0))
    att = lambda w: pl.BlockSpec((1, tm, w), lambda i, b: (b, blk0 + i, 0))
    mod_spec = pl.BlockSpec((1, 6, d), lambda i, b: (2 * b + mod_row, 0, 0))
    return pl.pallas_call(
        _post_kernel,
        grid=(n // tm, bsz),
        in_specs=[tok(d), mod_spec, att(384), att(256), att(384),
                  _const_spec(wout.shape), _const_spec((1, d)), _const_spec((1, d)), _const_spec((1, d)),
                  _const_spec(wg.shape), _const_spec(wu.shape), _const_spec(wd.shape)],
        out_specs=tok(d),
        out_shape=jax.ShapeDtypeStruct((bsz, n, d), jnp.float32),
        compiler_params=pltpu.CompilerParams(dimension_semantics=("arbitrary", "arbitrary"),
                                             vmem_limit_bytes=VMEM_LIMIT),
        name="post",
    )(x, mods, oa, ob, oc, wout, gpm, gpf, gqf, wg, wu, wd)


def kernel(x, c, ctx, c_ctx, w_ada, b_ada, g_pre_mix, g_post_mix, w_in, win_sink, diff_lambda_q1, diff_lambda_k1, diff_lambda_q2, diff_lambda_k2, diff_sub_norm, mla_q_norm, mla_w_q_up, mla_kv_norm, mla_w_kv_up, w_out, g_pre_ffn, g_post_ffn, w_gate, w_up, w_down):
    bsz, n_lat, d = x.shape
    n_ctx = ctx.shape[1]
    depth = w_ada.shape[0]
    assert d == D_MODEL and n_lat % TM_POST == 0 and n_ctx == TQ and n_lat >= BAND
    n_lat_blocks = n_lat // TQ
    bf = jnp.bfloat16

    cvec = jnp.zeros((8, d), jnp.float32).at[:bsz].set(c).at[bsz].set(c_ctx)
    mod_all = _ada_call(cvec, w_ada, b_ada)
    tables = _rope_tables(n_lat, n_ctx)

    ext_cols = _ext_columns()
    qup_cols = _qup_columns()
    kvup_cols = _kvup_columns()

    for l in range(depth):
        update_ctx = l < depth - 1
        m = mod_all[l].reshape(8, 6, d)
        mods = jnp.stack([m[:bsz], jnp.broadcast_to(m[bsz], (bsz, 6, d))], axis=1).reshape(2 * bsz, 6, d)

        w_ext = jnp.concatenate([w_in[l], jnp.zeros((d, 1), jnp.float32)], axis=1)[:, ext_cols].astype(bf)
        wq = jnp.concatenate([mla_w_q_up[l], jnp.zeros((C_Q_RANK, 1), jnp.float32)], axis=1)[:, qup_cols].astype(bf)
        wkv = mla_w_kv_up[l][:, kvup_cols].astype(bf)

        qa, ka, va, qb, kbt, vb, qc, kct, vc = _proj_call(
            x, ctx, mods, g_pre_mix[l][None], w_ext, mla_q_norm[l][None], wq, mla_kv_norm[l][None], wkv, tables)

        lam_init = 0.8 - 0.6 * math.exp(-0.3 * l)
        lam4 = jnp.stack([diff_lambda_q1[l], diff_lambda_k1[l], diff_lambda_q2[l], diff_lambda_k2[l]])
        sub = jnp.concatenate([diff_sub_norm[l], diff_sub_norm[l]])[None]

        nqb = n_lat_blocks + (1 if update_ctx else 0)
        oa = _attn_a_call(win_sink[l], qa, ka, va, n_lat, nqb)
        ob = _attn_b_call(lam4, sub, qb, kbt, vb, n_lat, lam_init, nqb)
        oc = _attn_c_call(qc, kct, vc, n_lat, nqb)

        weights = (w_out[l].astype(bf), g_post_mix[l][None], g_pre_ffn[l][None], g_post_ffn[l][None],
                   w_gate[l].astype(bf), w_up[l].astype(bf), w_down[l].astype(bf))
        x_new = _post_call(x, mods, oa, ob, oc, *weights, tm=TM_POST, blk0=0, mod_row=0)
        if update_ctx:
            ctx = _post_call(ctx, mods, oa, ob, oc, *weights, tm=TQ, blk0=n_lat_blocks, mod_row=1)
        x = x_new
    return x
```

```python
import functools
import math

import numpy as np
import jax
import jax.numpy as jnp
from jax import lax
from jax.experimental import pallas as pl
from jax.experimental.pallas import tpu as pltpu

D_MODEL = 1024
GRID_W = 64
HEAD_DIM = 64
A_HEADS = 6
A_KV_HEADS = 2
A_GROUP = A_HEADS // A_KV_HEADS
WINDOW = 128
B_HEADS = 4
B_QK_DIM = 32
B_V_DIM = 64
C_HEADS = 6
C_Q_RANK = 256
C_KV_RANK = 128
C_NOPE = 64
C_ROPE = 32
C_V = 64
D_FF = 2816
ROPE_BASE = 10000.0
NORM_EPS = 1e-6
NEG_INF = -1e30
LOG2E = math.log2(math.e)
A_SCALE = HEAD_DIM ** -0.5 * LOG2E
B_SCALE = B_QK_DIM ** -0.5 * LOG2E
C_SCALE = (C_NOPE + C_ROPE) ** -0.5 * LOG2E

LANES = 128
TQ = 256
TM_POST = 512
TK = 256
SUB = 11
LOOKAHEAD = 1
GROUP = 11
ROW_SUM_LIMIT = 2.0 ** 100
BAND = TQ + 2 * WINDOW
VMEM_LIMIT = 56 * 1024 * 1024

_AQ, _AK, _AV = 0, 384, 512
_DQ, _DK, _DV = 640, 896, 1152
_MQ, _MKV, _MKR = 1408, 1664, 1792
_ZERO_COL = 1824

N_ROPED = 384 + 256 + 256 + 256 + 128
N_PLAIN = 256 + 256 + 256 + 128
N_EXT = N_ROPED + N_PLAIN


def _rope_partner(unit):
    half = unit // 2
    quarter = half // 2
    d = np.arange(unit)
    r = d % half
    partner = np.where(r < quarter, d + quarter, d - quarter)
    sign = np.where(r < quarter, -1.0, 1.0).astype(np.float32)
    return partner, sign


def _ext_columns():
    qa = np.arange(A_HEADS * HEAD_DIM)
    dup = np.concatenate([np.arange(64), np.arange(64), 64 + np.arange(64), 64 + np.arange(64)])
    qb = np.arange(256)
    kr = np.concatenate([np.arange(32), np.arange(32)])
    zeros64 = np.full((64,), _ZERO_COL)
    main = [_AQ + qa, _AK + dup, _DQ + qb, _DK + qb, np.concatenate([_MKR + kr, zeros64])]
    plain = [_AV + dup, _DV + qb, _MQ + np.arange(256), _MKV + np.arange(128)]
    cols = np.concatenate(main + plain)
    assert cols.shape[0] == N_EXT
    return cols


def _qup_columns():
    pb, _ = _rope_partner(C_ROPE)
    width = C_NOPE + C_ROPE
    zero = C_HEADS * width
    zeros64 = np.full((64,), zero)
    main, rots = [], []
    for p in range(C_HEADS // 2):
        h0, h1 = 2 * p, 2 * p + 1
        main += [h0 * width + np.arange(64), h1 * width + np.arange(64),
                 h0 * width + 64 + np.arange(32), h1 * width + 64 + np.arange(32), zeros64]
        rots += [h0 * width + 64 + pb, h1 * width + 64 + pb, zeros64]
    return np.concatenate(main + rots)


def _kvup_columns():
    width = C_NOPE + C_V
    k = np.concatenate([h * width + np.arange(64) for h in range(C_HEADS)])
    v = np.concatenate([h * width + 64 + np.arange(64) for h in range(C_HEADS)])
    return np.concatenate([k, v])


def _rope_tables(n_lat, n_ctx):
    t = jnp.arange(n_lat)
    row = (t // GRID_W).astype(jnp.float32)
    col = (t % GRID_W).astype(jnp.float32)

    def table(unit):
        half = unit // 2
        quarter = half // 2
        _, sign = _rope_partner(unit)
        lane = np.arange(LANES)
        d = lane % unit
        axis = d // half
        j = d % quarter
        freqs = ROPE_BASE ** (-jnp.arange(0, half, 2, dtype=jnp.float32) / half)
        f = freqs[j]
        pos = jnp.where(jnp.asarray(axis)[None, :] == 0, row[:, None], col[:, None])
        ang = pos * f[None, :]
        cos = jnp.cos(ang)
        sin = jnp.sin(ang) * jnp.asarray(sign[d])[None, :]
        cos = jnp.concatenate([cos, jnp.ones((n_ctx, LANES), jnp.float32)], axis=0)
        sin = jnp.concatenate([sin, jnp.zeros((n_ctx, LANES), jnp.float32)], axis=0)
        return cos, sin

    cos_a, sin_a = table(HEAD_DIM)
    cos_b, sin_b = table(B_QK_DIM)
    return cos_a, sin_a, cos_b, sin_b


def _rms(x, g):
    return x * lax.rsqrt(jnp.mean(x * x, axis=-1, keepdims=True) + NORM_EPS) * g


def _dot(a, b):
    return jnp.dot(a, b, preferred_element_type=jnp.float32)


def _dot_nt(a, b):
    return lax.dot_general(a, b, (((1,), (1,)), ((), ())), preferred_element_type=jnp.float32)


def _ada_kernel(c_ref, w_ref, b_ref, o_ref):
    c = c_ref[...]
    h = c * jax.nn.sigmoid(c)
    o_ref[0] = _dot(h, w_ref[0]) + b_ref[0]


def _ada_call(cvec, w_ada, b_ada):
    depth, d, n = w_ada.shape
    tn = 1536
    return pl.pallas_call(
        _ada_kernel,
        grid=(depth, n // tn),
        in_specs=[pl.BlockSpec((8, d), lambda l, j: (0, 0)),
                  pl.BlockSpec((1, d, tn), lambda l, j: (l, 0, j)),
                  pl.BlockSpec((1, 1, tn), lambda l, j: (l, 0, j))],
        out_specs=pl.BlockSpec((1, 8, tn), lambda l, j: (l, 0, j)),
        out_shape=jax.ShapeDtypeStruct((depth, 8, n), jnp.float32),
        compiler_params=pltpu.CompilerParams(dimension_semantics=("arbitrary", "arbitrary"),
                                             vmem_limit_bytes=VMEM_LIMIT),
        name="adaln",
    )(cvec, w_ada, b_ada.reshape(depth, 1, n))


def _proj_kernel(x_ref, ctx_ref, mod_ref, gpre_ref, w_ref, qn_ref, wq_ref, kvn_ref, wkv_ref,
                 ca_ref, sa_ref, cb_ref, sb_ref,
                 qa_ref, ka_ref, va_ref, qb_ref, kbt_ref, vb_ref, qc_ref, kct_ref, vc_ref, *, n_lat_blocks):
    mod = mod_ref[0]
    x = jnp.where(pl.program_id(1) == n_lat_blocks, ctx_ref[0], x_ref[0])
    h = _rms(x, gpre_ref[...]) * (1.0 + mod[1:2]) + mod[0:1]
    p = _dot(h.astype(jnp.bfloat16), w_ref[...])

    ca, sa, cb, sb = ca_ref[...], sa_ref[...], cb_ref[...], sb_ref[...]

    lane = lax.broadcasted_iota(jnp.int32, (TQ, LANES), 1)

    def roped(off, width, unit, cos, sin):
        quarter = unit // 4
        first = (lane % (unit // 2)) < quarter
        out = []
        for k in range(width // LANES):
            main = p[:, off + k * LANES:off + (k + 1) * LANES]
            partner = jnp.where(first, pltpu.roll(main, LANES - quarter, axis=1), pltpu.roll(main, quarter, axis=1))
            out.append(main * cos + partner * sin)
        return jnp.concatenate(out, axis=1)

    qa_ref[0] = (roped(0, 384, HEAD_DIM, ca, sa) * A_SCALE).astype(jnp.bfloat16)
    ka_ref[0] = roped(384, 256, HEAD_DIM, ca, sa).astype(jnp.bfloat16)
    qb_ref[0] = (roped(640, 256, B_QK_DIM, cb, sb) * B_SCALE).astype(jnp.bfloat16)
    kbt_ref[0, 0] = roped(896, 256, B_QK_DIM, cb, sb).T.astype(jnp.bfloat16)
    kr = roped(1152, 128, C_ROPE, cb, sb)

    o = N_ROPED
    va_ref[0] = p[:, o:o + 256].astype(jnp.bfloat16)
    vb_ref[0] = p[:, o + 256:o + 512].astype(jnp.bfloat16)
    cq = p[:, o + 512:o + 768]
    ckv = p[:, o + 768:o + 896]

    qc = _dot(_rms(cq, qn_ref[...]).astype(jnp.bfloat16), wq_ref[...])
    kv = _dot(_rms(ckv, kvn_ref[...]).astype(jnp.bfloat16), wkv_ref[...])
    vc_ref[0] = kv[:, 384:].astype(jnp.bfloat16)
    kr_t = kr.T.astype(jnp.bfloat16)
    for pr in range(C_HEADS // 2):
        nope = qc[:, 256 * pr:256 * pr + 128]
        rope = (qc[:, 256 * pr + 128:256 * pr + 256] * cb
                + qc[:, 768 + 128 * pr:768 + 128 * pr + 128] * sb)
        qc_ref[0, :, 256 * pr:256 * pr + 128] = (nope * C_SCALE).astype(jnp.bfloat16)
        qc_ref[0, :, 256 * pr + 128:256 * pr + 256] = (rope * C_SCALE).astype(jnp.bfloat16)
        kct_ref[0, 0, 256 * pr:256 * pr + 128, :] = kv[:, 128 * pr:128 * pr + 128].T.astype(jnp.bfloat16)
        kct_ref[0, 0, 256 * pr + 128:256 * pr + 256, :] = kr_t


def _const_spec(shape):
    nd = len(shape)
    return pl.BlockSpec(shape, lambda *_: (0,) * nd, pipeline_mode=pl.Buffered(1))


def _proj_call(x, ctx, mods, gpre, w_ext, qn, wq, kvn, wkv, tables):
    bsz, n_lat, d = x.shape
    n_lat_blocks = n_lat // TQ
    nb = n_lat_blocks + 1
    t = nb * TQ
    tok = lambda w: pl.BlockSpec((1, TQ, w), lambda b, i: (b, i, 0))
    tbl = pl.BlockSpec((TQ, LANES), lambda b, i: (i, 0))
    kt = lambda r: pl.BlockSpec((1, 1, r, TQ), lambda b, i: (b, i, 0, 0))
    x_spec = pl.BlockSpec((1, TQ, d), lambda b, i: (b, jnp.minimum(i, n_lat_blocks - 1), 0))
    ctx_spec = pl.BlockSpec((1, TQ, d), lambda b, i: (b, 0, 0))
    mod_spec = pl.BlockSpec((1, 6, d), lambda b, i: (2 * b + (i >= n_lat_blocks).astype(jnp.int32), 0, 0))
    bf = jnp.bfloat16
    out_shape = [jax.ShapeDtypeStruct((bsz, t, 384), bf), jax.ShapeDtypeStruct((bsz, t, 256), bf),
                 jax.ShapeDtypeStruct((bsz, t, 256), bf), jax.ShapeDtypeStruct((bsz, t, 256), bf),
                 jax.ShapeDtypeStruct((bsz, nb, 256, TQ), bf), jax.ShapeDtypeStruct((bsz, t, 256), bf),
                 jax.ShapeDtypeStruct((bsz, t, 768), bf), jax.ShapeDtypeStruct((bsz, nb, 768, TQ), bf),
                 jax.ShapeDtypeStruct((bsz, t, 384), bf)]
    out_specs = [tok(384), tok(256), tok(256), tok(256), kt(256), tok(256), tok(768), kt(768), tok(384)]
    return pl.pallas_call(
        functools.partial(_proj_kernel, n_lat_blocks=n_lat_blocks),
        grid=(bsz, nb),
        in_specs=[x_spec, ctx_spec, mod_spec, _const_spec((1, d)), _const_spec(w_ext.shape),
                  _const_spec(qn.shape), _const_spec(wq.shape), _const_spec(kvn.shape), _const_spec(wkv.shape),
                  tbl, tbl, tbl, tbl],
        out_specs=out_specs,
        out_shape=out_shape,
        compiler_params=pltpu.CompilerParams(dimension_semantics=("arbitrary", "arbitrary"),
                                             vmem_limit_bytes=VMEM_LIMIT),
        name="proj",
    )(x, ctx, mods, gpre, w_ext, qn, wq, kvn, wkv, *tables)


def _attn_a_kernel(sink_ref, q_ref, k_ref, v_ref, o_ref, bias_ref, *, n_lat):
    i = pl.program_id(1)
    is_ctx = i == n_lat // TQ
    t0 = i * TQ
    start = pl.multiple_of(jnp.clip(t0 - WINDOW, 0, n_lat - BAND), WINDOW)
    kcat = jnp.concatenate([k_ref[0, pl.ds(start, BAND), :], k_ref[0, n_lat:, :]], axis=0)
    vcat = jnp.concatenate([v_ref[0, pl.ds(start, BAND), :], v_ref[0, n_lat:, :]], axis=0)
    qpos = t0 + jnp.where(is_ctx, 4 * n_lat, 0) + lax.broadcasted_iota(jnp.int32, (TQ, BAND), 0)
    kpos = start + lax.broadcasted_iota(jnp.int32, (TQ, BAND), 1)
    bias_ref[...] = jnp.where(jnp.abs(qpos - kpos) <= WINDOW, 0.0, NEG_INF)
    low_half = lax.broadcasted_iota(jnp.int32, (TQ, LANES), 1) < HEAD_DIM

    def scores(head):
        qg = q_ref[0, :, LANES * (head // 2):LANES * (head // 2 + 1)]
        kv = head // A_GROUP
        qm = jnp.where(low_half if head % 2 == 0 else jnp.logical_not(low_half), qg, jnp.zeros_like(qg))
        return _dot_nt(qm, kcat[:, LANES * kv:LANES * (kv + 1)])

    outs = []
    nxt = scores(0)
    for head in range(A_HEADS):
        s = nxt
        if head + 1 < A_HEADS:
            nxt = scores(head + 1)
        kv = head // A_GROUP
        n_parts = s.shape[1] // LANES
        parts = [s[:, k * LANES:(k + 1) * LANES] for k in range(n_parts)]
        parts = [part + bias_ref[:, k * LANES:(k + 1) * LANES] if k < BAND // LANES else part
                 for k, part in enumerate(parts)]
        sink = sink_ref[head] * LOG2E
        m = jnp.maximum(jnp.max(functools.reduce(jnp.maximum, parts), axis=1, keepdims=True), sink)
        ps = [jnp.exp2(part - m) for part in parts]
        l = jnp.sum(functools.reduce(jnp.add, ps), axis=1, keepdims=True) + jnp.exp2(sink - m)
        pb = jnp.concatenate([p.astype(jnp.bfloat16) for p in ps], axis=1)
        outs.append(_dot(pb, vcat[:, LANES * kv:LANES * (kv + 1)]) / l)
        if head % 2 == 1:
            pr = head // 2
            o_ref[0, :, LANES * pr:LANES * (pr + 1)] = jnp.where(low_half, outs[-2], outs[-1]).astype(o_ref.dtype)


def _attn_a_call(sink, qa, ka, va, n_lat, nqb):
    bsz, t, _ = qa.shape
    return pl.pallas_call(
        functools.partial(_attn_a_kernel, n_lat=n_lat),
        grid=(bsz, nqb),
        in_specs=[pl.BlockSpec(memory_space=pltpu.SMEM),
                  pl.BlockSpec((1, TQ, 384), lambda b, i: (b, i, 0)),
                  pl.BlockSpec((1, t, 256), lambda b, i: (b, 0, 0)),
                  pl.BlockSpec((1, t, 256), lambda b, i: (b, 0, 0))],
        out_specs=pl.BlockSpec((1, TQ, 384), lambda b, i: (b, i, 0)),
        out_shape=jax.ShapeDtypeStruct((bsz, nqb * TQ, 384), jnp.bfloat16),
        scratch_shapes=[pltpu.VMEM((TQ, BAND), jnp.float32)],
        compiler_params=pltpu.CompilerParams(dimension_semantics=("arbitrary", "arbitrary"),
                                             vmem_limit_bytes=VMEM_LIMIT),
        name="attn_a",
    )(sink, qa, ka, va)


def _flash_init(m_ref, l_ref, acc_ref):
    m_ref[...] = jnp.full(m_ref.shape, NEG_INF, jnp.float32)
    l_ref[...] = jnp.zeros(l_ref.shape, jnp.float32)
    acc_ref[...] = jnp.zeros(acc_ref.shape, jnp.float32)


def _flash_scores(qm, kts):
    parts = []
    for kt in kts:
        s = _dot(qm, kt)
        parts += [s[:, k * LANES:(k + 1) * LANES] for k in range(TK // LANES)]
    return parts


def _flash_update(j, m_ref, l_ref, acc_ref, parts, v):
    m_old = m_ref[j]
    m_new = jnp.maximum(m_old, jnp.max(functools.reduce(jnp.maximum, parts), axis=1, keepdims=True))
    alpha = jnp.exp2(m_old - m_new)
    ps = [jnp.exp2(part - m_new) for part in parts]
    l_ref[j] = alpha * l_ref[j] + functools.reduce(jnp.add, ps)
    pb = jnp.concatenate([p.astype(jnp.bfloat16) for p in ps], axis=1)
    acc_ref[j] = alpha * acc_ref[j] + _dot(pb, v)
    m_ref[j] = m_new


def _key_steps(n_tiles, size):
    return [(c, min(size, n_tiles - c)) for c in range(0, n_tiles, size)]


def _flash_run(m_ref, l_ref, acc_ref, qms, steps, kts_of, v_of):
    seq = [(c0, n, j) for (c0, n) in steps for j in range(len(qms))]
    scores = lambda item: _flash_scores(qms[item[2]], kts_of(*item))
    pending = [scores(item) for item in seq[:LOOKAHEAD]]
    for idx, (c0, n, j) in enumerate(seq):
        if idx + LOOKAHEAD < len(seq):
            pending.append(scores(seq[idx + LOOKAHEAD]))
        _flash_update(j, m_ref, l_ref, acc_ref, pending.pop(0), v_of(c0, n, j))


def _single_pass_run(acc_ref, qms, groups, kts_of, v_of):
    seq = [(j, gi) for j in range(len(qms)) for gi in range(len(groups))]
    scores = lambda item: _flash_scores(qms[item[0]], kts_of(*groups[item[1]], item[0]))
    pending = [scores(seq[0])]
    l_max = None
    for idx, (j, gi) in enumerate(seq):
        if idx + 1 < len(seq):
            pending.append(scores(seq[idx + 1]))
        parts = pending.pop(0)
        if gi == 0:
            shift = jnp.max(functools.reduce(jnp.maximum, parts[:TK // LANES]), axis=1, keepdims=True)
            l = acc = None
        ps = [jnp.exp2(part - shift) for part in parts]
        l_g = functools.reduce(jnp.add, ps)
        pb = jnp.concatenate([p.astype(jnp.bfloat16) for p in ps], axis=1)
        pv = _dot(pb, v_of(*groups[gi], j))
        l, acc = (l_g, pv) if gi == 0 else (l + l_g, acc + pv)
        if gi == len(groups) - 1:
            l_row = jnp.sum(l, axis=1, keepdims=True)
            acc_ref[j] = acc / l_row
            l_max = jnp.max(l_row) if l_max is None else jnp.maximum(l_max, jnp.max(l_row))
    return l_max


def _diff_single_pass_run(acc_ref, p_ref, qms, groups, kts_of, v_of, lam):
    n_g = len(groups)
    tasks = []
    for h in range(len(qms) // 2):
        tasks += [("scores", h, m, gi) for m in range(2) for gi in range(n_g)]
        tasks += [("values", h, 0, gi) for gi in range(n_g)]
    score_tasks = [t for t in tasks if t[0] == "scores"]
    pending = []

    def issue():
        if score_tasks:
            _, h, m, gi = score_tasks.pop(0)
            pending.append(_flash_scores(qms[2 * h + m], kts_of(*groups[gi], 2 * h + m)))

    issue()
    l_max = None
    shift, l = [None, None], [None, None]
    for kind, h, m, gi in tasks:
        c0, n = groups[gi]
        keys = slice(c0 * TK, (c0 + n) * TK)
        if kind == "scores":
            parts = pending.pop(0)
            issue()
            if gi == 0:
                shift[m] = jnp.max(functools.reduce(jnp.maximum, parts[:TK // LANES]), axis=1, keepdims=True)
            ps = [jnp.exp2(part - shift[m]) for part in parts]
            l_g = functools.reduce(jnp.add, ps)
            l[m] = l_g if gi == 0 else l[m] + l_g
            p_ref[2 * (h % 2) + m, :, keys] = jnp.concatenate([p.astype(jnp.bfloat16) for p in ps], axis=1)
        else:
            if gi == 0:
                l0 = jnp.sum(l[0], axis=1, keepdims=True)
                l1 = jnp.sum(l[1], axis=1, keepdims=True)
                coef = jnp.broadcast_to(lam * l0 / l1, (TQ, LANES)).astype(jnp.bfloat16)
                big = jnp.maximum(jnp.max(l0), jnp.max(l1))
                l_max = big if l_max is None else jnp.maximum(l_max, big)
            a = (p_ref[2 * (h % 2), :, keys]
                 - jnp.concatenate([coef] * (n * TK // LANES), axis=1) * p_ref[2 * (h % 2) + 1, :, keys])
            pv = _dot(a, v_of(c0, n, 2 * h))
            acc = pv if gi == 0 else acc + pv
            if gi == n_g - 1:
                acc_ref[h] = acc / l0
    return l_max


def _online_softmax(m_ref, l_ref, acc_ref, qms, steps, kts_of, v_of):
    _flash_init(m_ref, l_ref, acc_ref)
    _flash_run(m_ref, l_ref, acc_ref, qms, steps, kts_of, v_of)
    for j in range(len(qms)):
        acc_ref[j] = acc_ref[j] / jnp.sum(l_ref[j], axis=1, keepdims=True)


def _attend(fast, exact, is_ctx, n_lat_tiles):
    @pl.when(is_ctx)
    def _():
        fast([(n_lat_tiles, 1)])

    @pl.when(jnp.logical_not(is_ctx))
    def _():
        l_max = fast(_key_steps(n_lat_tiles + 1, GROUP))

        @pl.when(jnp.logical_not(l_max < ROW_SUM_LIMIT))
        def _():
            exact(_key_steps(n_lat_tiles + 1, SUB))


def _resident_spec(shape):
    nd = len(shape)
    return pl.BlockSpec(shape, lambda b, i: (b,) + (0,) * (nd - 1), pipeline_mode=pl.Buffered(1))


def _flash_scratch(n_maps):
    return [pltpu.VMEM((n_maps, TQ, LANES), jnp.float32) for _ in range(3)]


def _attn_b_kernel(lam_ref, sub_ref, q_ref, kt_ref, v_ref, o_ref, m_ref, l_ref, acc_ref, p_ref, *,
                   n_lat, lam_init):
    i = pl.program_id(1)
    n_lat_tiles = n_lat // TK
    is_ctx = i == n_lat // TQ
    lam4 = lam_ref[...]
    lam = (jnp.exp(jnp.sum(lam4[0:1] * lam4[1:2], axis=-1, keepdims=True))
           - jnp.exp(jnp.sum(lam4[2:3] * lam4[3:4], axis=-1, keepdims=True)) + lam_init)

    lane = lax.broadcasted_iota(jnp.int32, (TQ, LANES), 1)
    n_maps = 2 * B_HEADS
    qms = []
    for j in range(n_maps):
        qg = q_ref[0, :, LANES * (j // 4):LANES * (j // 4 + 1)]
        qms.append(jnp.where((lane // B_QK_DIM) == (j % 4), qg, jnp.zeros_like(qg)))

    def kts_of(c0, n, j):
        g = j // 4
        return [kt_ref[0, c0 + k, LANES * g:LANES * (g + 1), :] for k in range(n)]

    def v_of(c0, n, j):
        g = j // 4
        return v_ref[0, c0 * TK:(c0 + n) * TK, LANES * g:LANES * (g + 1)]

    def exact(steps):
        _online_softmax(m_ref, l_ref, acc_ref, qms, steps, kts_of, v_of)
        for h in range(B_HEADS):
            acc_ref[h] = acc_ref[2 * h] - lam * acc_ref[2 * h + 1]

    _attend(lambda groups: _diff_single_pass_run(acc_ref, p_ref, qms, groups, kts_of, v_of, lam),
            exact, is_ctx, n_lat_tiles)

    low_half = lane < B_V_DIM
    for g in range(2):
        o = jnp.where(low_half, acc_ref[2 * g], acc_ref[2 * g + 1])
        sq = o * o
        ms_lo = jnp.sum(jnp.where(low_half, sq, 0.0), axis=-1, keepdims=True) * (1.0 / B_V_DIM)
        ms_hi = jnp.sum(jnp.where(low_half, 0.0, sq), axis=-1, keepdims=True) * (1.0 / B_V_DIM)
        ms = jnp.where(low_half, ms_lo, ms_hi)
        y = o * lax.rsqrt(ms + NORM_EPS) * sub_ref[...] * (1.0 - lam_init)
        o_ref[0, :, LANES * g:LANES * (g + 1)] = y.astype(o_ref.dtype)


def _attn_b_call(lam4, sub, qb, kbt, vb, n_lat, lam_init, nqb):
    bsz, t, _ = qb.shape
    nb = t // TQ
    return pl.pallas_call(
        functools.partial(_attn_b_kernel, n_lat=n_lat, lam_init=lam_init),
        grid=(bsz, nqb),
        in_specs=[pl.BlockSpec((4, B_QK_DIM), lambda b, i: (0, 0)),
                  pl.BlockSpec((1, LANES), lambda b, i: (0, 0)),
                  pl.BlockSpec((1, TQ, 256), lambda b, i: (b, i, 0)),
                  _resident_spec((1, nb, 256, TK)), _resident_spec((1, t, 256))],
        out_specs=pl.BlockSpec((1, TQ, 256), lambda b, i: (b, i, 0)),
        out_shape=jax.ShapeDtypeStruct((bsz, nqb * TQ, 256), jnp.bfloat16),
        scratch_shapes=_flash_scratch(2 * B_HEADS) + [pltpu.VMEM((4, TQ, t), jnp.bfloat16)],
        compiler_params=pltpu.CompilerParams(dimension_semantics=("arbitrary", "arbitrary"),
                                             vmem_limit_bytes=VMEM_LIMIT),
        name="attn_b",
    )(lam4, sub, qb, kbt, vb)


def _attn_c_kernel(q_ref, kt_ref, v_ref, o_ref, m_ref, l_ref, acc_ref, *, n_lat):
    i = pl.program_id(1)
    n_lat_tiles = n_lat // TK
    is_ctx = i == n_lat // TQ
    lane2 = lax.broadcasted_iota(jnp.int32, (TQ, 2 * LANES), 1)
    even = (lane2 < 64) | ((lane2 >= 128) & (lane2 < 160))
    odd = ((lane2 >= 64) & (lane2 < 128)) | ((lane2 >= 160) & (lane2 < 192))
    qms = []
    for h in range(C_HEADS):
        qp = q_ref[0, :, 256 * (h // 2):256 * (h // 2 + 1)]
        qms.append(jnp.where(even if h % 2 == 0 else odd, qp, jnp.zeros_like(qp)))

    def kts_of(c0, n, h):
        pr = h // 2
        return [kt_ref[0, c0 + k, 256 * pr:256 * (pr + 1), :] for k in range(n)]

    def v_of(c0, n, h):
        pr = h // 2
        return v_ref[0, c0 * TK:(c0 + n) * TK, LANES * pr:LANES * (pr + 1)]

    _attend(lambda groups: _single_pass_run(acc_ref, qms, groups, kts_of, v_of),
            lambda steps: _online_softmax(m_ref, l_ref, acc_ref, qms, steps, kts_of, v_of),
            is_ctx, n_lat_tiles)

    low_half = lax.broadcasted_iota(jnp.int32, (TQ, LANES), 1) < C_V
    for pr in range(C_HEADS // 2):
        o = jnp.where(low_half, acc_ref[2 * pr], acc_ref[2 * pr + 1])
        o_ref[0, :, LANES * pr:LANES * (pr + 1)] = o.astype(o_ref.dtype)


def _attn_c_call(qc, kct, vc, n_lat, nqb):
    bsz, t, _ = qc.shape
    nb = t // TQ
    return pl.pallas_call(
        functools.partial(_attn_c_kernel, n_lat=n_lat),
        grid=(bsz, nqb),
        in_specs=[pl.BlockSpec((1, TQ, 768), lambda b, i: (b, i, 0)),
                  _resident_spec((1, nb, 768, TK)), _resident_spec((1, t, 384))],
        out_specs=pl.BlockSpec((1, TQ, 384), lambda b, i: (b, i, 0)),
        out_shape=jax.ShapeDtypeStruct((bsz, nqb * TQ, 384), jnp.bfloat16),
        scratch_shapes=_flash_scratch(C_HEADS),
        compiler_params=pltpu.CompilerParams(dimension_semantics=("arbitrary", "arbitrary"),
                                             vmem_limit_bytes=VMEM_LIMIT),
        name="attn_c",
    )(qc, kct, vc)


def _post_kernel(x_ref, mod_ref, oa_ref, ob_ref, oc_ref, wout_ref, gpm_ref, gpf_ref, gqf_ref,
                 wg_ref, wu_ref, wd_ref, o_ref):
    mod = mod_ref[0]
    tm = x_ref.shape[1]
    nh = 2 if tm >= 2 * TQ else 1
    hs = [slice(k * (tm // nh), (k + 1) * (tm // nh)) for k in range(nh)]
    mixd = [_dot(jnp.concatenate([oa_ref[0, h], ob_ref[0, h], oc_ref[0, h]], axis=1), wout_ref[...]) for h in hs]
    xs, gs, us = [], [], []
    for k, h in enumerate(hs):
        x = x_ref[0, h] + mod[2:3] * _rms(mixd[k], gpm_ref[...])
        hb = (_rms(x, gpf_ref[...]) * (1.0 + mod[4:5]) + mod[3:4]).astype(jnp.bfloat16)
        xs.append(x)
        gs.append(_dot(hb, wg_ref[...]))
        us.append(_dot(hb, wu_ref[...]))
    ffs = []
    for k in range(nh):
        act = (gs[k] * jax.nn.sigmoid(gs[k]) * us[k]).astype(jnp.bfloat16)
        ffs.append(_dot(act, wd_ref[...]))
    for k, h in enumerate(hs):
        o_ref[0, h] = xs[k] + mod[5:6] * _rms(ffs[k], gqf_ref[...])


def _post_call(x, mods, oa, ob, oc, wout, gpm, gpf, gqf, wg, wu, wd, *, tm, blk0, mod_row):
    bsz, n, d = x.shape
    tok = lambda w: pl.BlockSpec((1, tm, w), lambda i, b: (b, i, 0))
    att = lambda w: pl.BlockSpec((1, tm, w), lambda i, b: (b, blk0 + i, 0))
    mod_spec = pl.BlockSpec((1, 6, d), lambda i, b: (2 * b + mod_row, 0, 0))
    return pl.pallas_call(
        _post_kernel,
        grid=(n // tm, bsz),
        in_specs=[tok(d), mod_spec, att(384), att(256), att(384),
                  _const_spec(wout.shape), _const_spec((1, d)), _const_spec((1, d)), _const_spec((1, d)),
                  _const_spec(wg.shape), _const_spec(wu.shape), _const_spec(wd.shape)],
        out_specs=tok(d),
        out_shape=jax.ShapeDtypeStruct((bsz, n, d), jnp.float32),
        compiler_params=pltpu.CompilerParams(dimension_semantics=("arbitrary", "arbitrary"),
                                             vmem_limit_bytes=VMEM_LIMIT),
        name="post",
    )(x, mods, oa, ob, oc, wout, gpm, gpf, gqf, wg, wu, wd)


def kernel(x, c, ctx, c_ctx, w_ada, b_ada, g_pre_mix, g_post_mix, w_in, win_sink, diff_lambda_q1, diff_lambda_k1, diff_lambda_q2, diff_lambda_k2, diff_sub_norm, mla_q_norm, mla_w_q_up, mla_kv_norm, mla_w_kv_up, w_out, g_pre_ffn, g_post_ffn, w_gate, w_up, w_down):
    bsz, n_lat, d = x.shape
    n_ctx = ctx.shape[1]
    depth = w_ada.shape[0]
    assert d == D_MODEL and n_lat % TM_POST == 0 and n_ctx == TQ and n_lat >= BAND
    n_lat_blocks = n_lat // TQ
    bf = jnp.bfloat16

    cvec = jnp.zeros((8, d), jnp.float32).at[:bsz].set(c).at[bsz].set(c_ctx)
    mod_all = _ada_call(cvec, w_ada, b_ada)
    tables = _rope_tables(n_lat, n_ctx)

    ext_cols = _ext_columns()
    qup_cols = _qup_columns()
    kvup_cols = _kvup_columns()

    for l in range(depth):
        update_ctx = l < depth - 1
        m = mod_all[l].reshape(8, 6, d)
        mods = jnp.stack([m[:bsz], jnp.broadcast_to(m[bsz], (bsz, 6, d))], axis=1).reshape(2 * bsz, 6, d)

        w_ext = jnp.concatenate([w_in[l], jnp.zeros((d, 1), jnp.float32)], axis=1)[:, ext_cols].astype(bf)
        wq = jnp.concatenate([mla_w_q_up[l], jnp.zeros((C_Q_RANK, 1), jnp.float32)], axis=1)[:, qup_cols].astype(bf)
        wkv = mla_w_kv_up[l][:, kvup_cols].astype(bf)

        qa, ka, va, qb, kbt, vb, qc, kct, vc = _proj_call(
            x, ctx, mods, g_pre_mix[l][None], w_ext, mla_q_norm[l][None], wq, mla_kv_norm[l][None], wkv, tables)

        lam_init = 0.8 - 0.6 * math.exp(-0.3 * l)
        lam4 = jnp.stack([diff_lambda_q1[l], diff_lambda_k1[l], diff_lambda_q2[l], diff_lambda_k2[l]])
        sub = jnp.concatenate([diff_sub_norm[l], diff_sub_norm[l]])[None]

        nqb = n_lat_blocks + (1 if update_ctx else 0)
        oa = _attn_a_call(win_sink[l], qa, ka, va, n_lat, nqb)
        ob = _attn_b_call(lam4, sub, qb, kbt, vb, n_lat, lam_init, nqb)
        oc = _attn_c_call(qc, kct, vc, n_lat, nqb)

        weights = (w_out[l].astype(bf), g_post_mix[l][None], g_pre_ffn[l][None], g_post_ffn[l][None],
                   w_gate[l].astype(bf), w_up[l].astype(bf), w_down[l].astype(bf))
        x_new = _post_call(x, mods, oa, ob, oc, *weights, tm=TM_POST, blk0=0, mod_row=0)
        if update_ctx:
            ctx = _post_call(ctx, mods, oa, ob, oc, *weights, tm=TQ, blk0=n_lat_blocks, mod_row=1)
        x = x_new
    return x
```
